```python
import math
import jax, jax.numpy as jnp
from jax import lax
import numpy as np

D_MODEL = 4096
BATCH = 2
SEQ = 8192
DEPTH = 4

N_MIXERS = 2
N_A_LAYERS = (DEPTH + 1) // 2
N_B_LAYERS = DEPTH // 2
CHUNK = 128
A_WIDTH = D_MODEL
A_HEADS = 16
A_HEAD_DIM = A_WIDTH // A_HEADS
B_WIDTH = D_MODEL
B_HEADS = 16
B_HEAD_DIM = B_WIDTH // B_HEADS
CONV_WIDTH = 4
LRU_C = 8.0
N_EXPERTS = 32
TOP_K = 4
EXPERT_FF = (3 * D_MODEL) // 32
SWIGLU_LIMIT = 7.0
SWIGLU_ALPHA = 1.702
MOE_BLOCK = 128
LN_EPS = 1e-5
DEEPNORM_ALPHA = (2 * DEPTH) ** 0.25
DEEPNORM_BETA = (8 * DEPTH) ** -0.25

kernel_name = "hybrid_gmlp_rglru_moe_deepnorm"


def layer_norm(x, g, b):
    xf = x.astype(jnp.float32)
    mu = jnp.mean(xf, axis=-1, keepdims=True)
    xc = xf - mu
    var = jnp.mean(xc * xc, axis=-1, keepdims=True)
    y = xc * lax.rsqrt(var + LN_EPS) * g.astype(jnp.float32) + b.astype(jnp.float32)
    return y.astype(x.dtype)


def chunked_gmlp(x, w_in, ln_g, ln_b, w_s, b_s, w_out):
    bsz, seq, _ = x.shape
    z = jax.nn.gelu(x @ w_in)
    u, v = jnp.split(z, 2, axis=-1)
    v = layer_norm(v, ln_g, ln_b)
    v = v.reshape(bsz, seq // CHUNK, CHUNK, A_HEADS, A_HEAD_DIM)
    causal = jnp.tril(jnp.ones((CHUNK, CHUNK), dtype=w_s.dtype))
    w = w_s * causal
    sv = jnp.einsum('hts,bcshd->bcthd', w, v) + b_s.T[None, None, :, :, None]
    sv = sv.reshape(bsz, seq, A_WIDTH)
    return (u * sv) @ w_out


def _lru_combine(left, right):
    a1, b1 = left
    a2, b2 = right
    return a1 * a2, a2 * b1 + b2


def rglru_block(x, w_in, conv_w, conv_b, w_r, b_r, w_i, b_i, lam, w_out):
    bsz, seq, _ = x.shape
    y, xr = jnp.split(x @ w_in, 2, axis=-1)
    xp = jnp.pad(xr, ((0, 0), (CONV_WIDTH - 1, 0), (0, 0)))
    xc = conv_b
    for k in range(CONV_WIDTH):
        xc = xc + xp[:, k:k + seq] * conv_w[k]
    xh = xc.reshape(bsz, seq, B_HEADS, B_HEAD_DIM)
    r = jax.nn.sigmoid(jnp.einsum('bshd,hde->bshe', xh, w_r) + b_r).reshape(bsz, seq, B_WIDTH)
    i = jax.nn.sigmoid(jnp.einsum('bshd,hde->bshe', xh, w_i) + b_i).reshape(bsz, seq, B_WIDTH)
    log_a = (-LRU_C * r.astype(jnp.float32)) * jax.nn.softplus(-lam.astype(jnp.float32))
    a = jnp.exp(log_a)
    mult = jnp.sqrt(-jnp.expm1(2.0 * log_a))
    bterm = mult * (i * xc).astype(jnp.float32)
    _, h = lax.associative_scan(_lru_combine, (a, bterm), axis=1)
    return (h.astype(x.dtype) * jax.nn.gelu(y)) @ w_out


def moe(x, router_w, router_b, w_gu, b_gu, w_down, b_down):
    bsz, seq, d = x.shape
    n_tok = bsz * seq
    x2d = x.reshape(n_tok, d)
    logits = x2d.astype(jnp.float32) @ router_w.astype(jnp.float32) + router_b.astype(jnp.float32)
    top_logit, top_e = lax.top_k(logits, TOP_K)
    gate = jax.nn.softmax(top_logit, axis=-1)
    n_assign = n_tok * TOP_K
    flat_e = top_e.reshape(n_assign).astype(jnp.int32)
    flat_tok = jnp.arange(n_assign, dtype=jnp.int32) // TOP_K
    flat_gate = gate.reshape(n_assign)
    order = jnp.argsort(flat_e)
    sorted_e = flat_e[order]
    counts = jnp.bincount(flat_e, length=N_EXPERTS)
    padded = (counts + MOE_BLOCK - 1) // MOE_BLOCK * MOE_BLOCK
    padded_end = jnp.cumsum(padded)
    padded_start = padded_end - padded
    start = jnp.cumsum(counts) - counts
    dest = padded_start[sorted_e] + jnp.arange(n_assign, dtype=jnp.int32) - start[sorted_e]
    n_blocks = -(-(n_assign + N_EXPERTS * (MOE_BLOCK - 1)) // MOE_BLOCK)
    n_rows = n_blocks * MOE_BLOCK
    row_tok = jnp.zeros((n_rows,), jnp.int32).at[dest].set(flat_tok[order])
    row_gate = jnp.zeros((n_rows,), jnp.float32).at[dest].set(flat_gate[order])
    block_start = jnp.arange(n_blocks, dtype=jnp.int32) * MOE_BLOCK
    block_e = jnp.minimum(jnp.searchsorted(padded_end, block_start, side='right'), N_EXPERTS - 1)

    def block_step(acc, blk):
        rows, g, e = blk
        xb = x2d[rows]
        gu = xb @ w_gu[e] + b_gu[e]
        hg, hu = gu[:, :EXPERT_FF], gu[:, EXPERT_FF:]
        hg = jnp.minimum(hg, SWIGLU_LIMIT)
        hu = jnp.clip(hu, -SWIGLU_LIMIT, SWIGLU_LIMIT)
        hdn = hg * jax.nn.sigmoid(SWIGLU_ALPHA * hg) * (hu + 1.0)
        yb = hdn @ w_down[e] + b_down[e]
        acc = acc.at[rows].add(yb.astype(jnp.float32) * g[:, None])
        return acc, None

    acc0 = jnp.zeros((n_tok, d), jnp.float32)
    out, _ = lax.scan(block_step, acc0,
                      (row_tok.reshape(n_blocks, MOE_BLOCK), row_gate.reshape(n_blocks, MOE_BLOCK), block_e))
    return out.astype(x.dtype).reshape(bsz, seq, d)


def setup_inputs(seed: int = 0) -> dict:
    key = jax.random.key(seed)
    ks = jax.random.split(key, 32)
    f32 = jnp.float32
    nrm = lambda k, shape, s: jax.random.normal(k, shape, f32) * s
    nA, nB = N_A_LAYERS, N_B_LAYERS
    u = jax.random.uniform(ks[13], (nB, B_WIDTH), f32, 0.9, 0.999)
    p = u ** (1.0 / LRU_C)
    lam = jnp.log(p) - jnp.log1p(-p)
    return {
        "x": jax.random.normal(ks[0], (BATCH, SEQ, D_MODEL), f32),
        "a_w_in": nrm(ks[1], (nA, D_MODEL, 2 * A_WIDTH), D_MODEL ** -0.5),
        "a_ln_g": 1.0 + nrm(ks[2], (nA, A_WIDTH), 0.02),
        "a_ln_b": nrm(ks[3], (nA, A_WIDTH), 0.02),
        "a_w_s": nrm(ks[4], (nA, A_HEADS, CHUNK, CHUNK), CHUNK ** -0.5),
        "a_b_s": 1.0 + nrm(ks[5], (nA, A_HEADS, CHUNK), 0.02),
        "a_w_out": nrm(ks[6], (nA, A_WIDTH, D_MODEL), A_WIDTH ** -0.5 * DEEPNORM_BETA),
        "b_w_in": nrm(ks[7], (nB, D_MODEL, 2 * B_WIDTH), D_MODEL ** -0.5),
        "b_conv_w": nrm(ks[8], (nB, CONV_WIDTH, B_WIDTH), CONV_WIDTH ** -0.5),
        "b_conv_b": nrm(ks[9], (nB, B_WIDTH), 0.02),
        "b_w_r": nrm(ks[10], (nB, B_HEADS, B_HEAD_DIM, B_HEAD_DIM), B_HEAD_DIM ** -0.5),
        "b_b_r": nrm(ks[11], (nB, B_HEADS, B_HEAD_DIM), 0.02),
        "b_w_i": nrm(ks[12], (nB, B_HEADS, B_HEAD_DIM, B_HEAD_DIM), B_HEAD_DIM ** -0.5),
        "b_b_i": nrm(ks[14], (nB, B_HEADS, B_HEAD_DIM), 0.02),
        "b_lambda": lam,
        "b_w_out": nrm(ks[15], (nB, B_WIDTH, D_MODEL), B_WIDTH ** -0.5 * DEEPNORM_BETA),
        "ln1_g": 1.0 + nrm(ks[16], (DEPTH, D_MODEL), 0.02),
        "ln1_b": nrm(ks[17], (DEPTH, D_MODEL), 0.02),
        "ln2_g": 1.0 + nrm(ks[18], (DEPTH, D_MODEL), 0.02),
        "ln2_b": nrm(ks[19], (DEPTH, D_MODEL), 0.02),
        "router_w": nrm(ks[20], (DEPTH, D_MODEL, N_EXPERTS), D_MODEL ** -0.5),
        "router_b": nrm(ks[21], (DEPTH, N_EXPERTS), 0.01),
        "ex_w_gu": nrm(ks[22], (DEPTH, N_EXPERTS, D_MODEL, 2 * EXPERT_FF), D_MODEL ** -0.5),
        "ex_b_gu": nrm(ks[23], (DEPTH, N_EXPERTS, 2 * EXPERT_FF), 0.02),
        "ex_w_down": nrm(ks[24], (DEPTH, N_EXPERTS, EXPERT_FF, D_MODEL), EXPERT_FF ** -0.5 * DEEPNORM_BETA),
        "ex_b_down": nrm(ks[25], (DEPTH, N_EXPERTS, D_MODEL), 0.02),
    }


def reference(x, a_w_in, a_ln_g, a_ln_b, a_w_s, a_b_s, a_w_out,
              b_w_in, b_conv_w, b_conv_b, b_w_r, b_b_r, b_w_i, b_b_i, b_lambda, b_w_out,
              ln1_g, ln1_b, ln2_g, ln2_b,
              router_w, router_b, ex_w_gu, ex_b_gu, ex_w_down, ex_b_down):
    h = x
    for layer in range(DEPTH):
        j = layer // N_MIXERS
        if layer % N_MIXERS == 0:
            mix = chunked_gmlp(h, a_w_in[j], a_ln_g[j], a_ln_b[j], a_w_s[j], a_b_s[j], a_w_out[j])
        else:
            mix = rglru_block(h, b_w_in[j], b_conv_w[j], b_conv_b[j], b_w_r[j], b_b_r[j],
                              b_w_i[j], b_b_i[j], b_lambda[j], b_w_out[j])
        h = layer_norm(DEEPNORM_ALPHA * h + mix, ln1_g[layer], ln1_b[layer])
        ffn = moe(h, router_w[layer], router_b[layer], ex_w_gu[layer], ex_b_gu[layer],
                  ex_w_down[layer], ex_b_down[layer])
        h = layer_norm(DEEPNORM_ALPHA * h + ffn, ln2_g[layer], ln2_b[layer])
    return h
```

```python
import functools

import jax
import jax.numpy as jnp
from jax import lax
from jax.experimental import pallas as pl
from jax.experimental.pallas import tpu as pltpu

F32 = jnp.float32
BF16 = jnp.bfloat16
I32 = jnp.int32

CHUNK = 128
N_HEADS = 16
CONV_WIDTH = 4
LRU_C = 8.0
TOP_K = 4
SWIGLU_LIMIT = 7.0
SWIGLU_ALPHA = 1.702
LN_EPS = 1e-5

LANES = 128
SUBLANES = 8
VMEM_LIMIT_BYTES = 60000 * 1024

NT_DIMS = (((1,), (1,)), ((), ()))


def _cparams(n_axes, vmem=VMEM_LIMIT_BYTES):
    return pltpu.CompilerParams(
        dimension_semantics=("arbitrary",) * n_axes, vmem_limit_bytes=vmem)


def _gelu(x):
    return 0.5 * x * (1.0 + jnp.tanh(0.7978845608028654 * (x + 0.044715 * (x * x * x))))


def _layer_norm(x, g, b):
    mu = jnp.mean(x, axis=-1, keepdims=True)
    xc = x - mu
    var = jnp.mean(xc * xc, axis=-1, keepdims=True)
    return xc * lax.rsqrt(var + LN_EPS) * g + b


def _bdot(a, b):
    return jnp.dot(a, b, preferred_element_type=F32)


def _proj_kernel(x_ref, w_ref, o_ref, *, gelu_panels):
    acc = _bdot(x_ref[...], w_ref[...])
    j = pl.program_id(0)

    @pl.when(j < gelu_panels)
    def _():
        o_ref[...] = _gelu(acc).astype(o_ref.dtype)

    @pl.when(j >= gelu_panels)
    def _():
        o_ref[...] = acc.astype(o_ref.dtype)


def _proj(x_bf, w_bf, n_gelu, tm, tn):
    t, k = x_bf.shape
    n = w_bf.shape[1]
    tm, tn = min(tm, t), min(tn, n)
    assert t % tm == 0 and n % tn == 0 and n_gelu % tn == 0
    return pl.pallas_call(
        functools.partial(_proj_kernel, gelu_panels=n_gelu // tn),
        grid=(n // tn, t // tm),
        in_specs=[pl.BlockSpec((tm, k), lambda j, i: (i, 0)),
                  pl.BlockSpec((k, tn), lambda j, i: (0, j))],
        out_specs=pl.BlockSpec((tm, tn), lambda j, i: (i, j)),
        out_shape=jax.ShapeDtypeStruct((t, n), BF16),
        compiler_params=_cparams(2),
        name="proj",
    )(x_bf, w_bf)


def _sgu_kernel(u_ref, v_ref, g_ref, b_ref, ws_ref, bs_ref, o_ref, *, n_chunks, hd):
    row = lax.broadcasted_iota(I32, (CHUNK, CHUNK), 0)
    col = lax.broadcasted_iota(I32, (CHUNK, CHUNK), 1)
    causal = row >= col
    for c in range(n_chunks):
        rows = slice(c * CHUNK, (c + 1) * CHUNK)
        vn = _layer_norm(v_ref[rows, :].astype(F32), g_ref[...], b_ref[...]).astype(BF16)
        for h in range(N_HEADS):
            cols = slice(h * hd, (h + 1) * hd)
            wm = jnp.where(causal, ws_ref[h], 0.0).astype(BF16)
            sv = _bdot(wm, vn[:, cols]) + bs_ref[:, h:h + 1]
            o_ref[rows, cols] = (u_ref[rows, cols].astype(F32) * sv).astype(o_ref.dtype)


def _sgu(z_bf, ln_g, ln_b, w_s, b_s, rows_per_step):
    t, w2 = z_bf.shape
    w = w2 // 2
    hd = w // N_HEADS
    r = min(rows_per_step, t)
    assert t % r == 0 and r % CHUNK == 0
    return pl.pallas_call(
        functools.partial(_sgu_kernel, n_chunks=r // CHUNK, hd=hd),
        grid=(t // r,),
        in_specs=[pl.BlockSpec((r, w), lambda i: (i, 0)),
                  pl.BlockSpec((r, w), lambda i: (i, 1)),
                  pl.BlockSpec((1, w), lambda i: (0, 0)),
                  pl.BlockSpec((1, w), lambda i: (0, 0)),
                  pl.BlockSpec((N_HEADS, CHUNK, CHUNK), lambda i: (0, 0, 0)),
                  pl.BlockSpec((CHUNK, N_HEADS), lambda i: (0, 0))],
        out_specs=pl.BlockSpec((r, w), lambda i: (i, 0)),
        out_shape=jax.ShapeDtypeStruct((t, w), BF16),
        compiler_params=_cparams(1),
        name="sgu",
    )(z_bf, z_bf, ln_g.reshape(1, w), ln_b.reshape(1, w), w_s, b_s.T)


def _shift_rows(x, tail, s):
    rolled = pltpu.roll(x, s, 0)
    tail_rolled = pltpu.roll(tail, s, 0)
    first = jnp.where(lax.broadcasted_iota(I32, tail.shape, 0) < s, tail_rolled, rolled[:SUBLANES])
    return jnp.concatenate([first, rolled[SUBLANES:]], axis=0)


def _rglru_kernel(y_ref, x_ref, cw_ref, cb_ref, wr_ref, br_ref, wi_ref, bi_ref, lam_ref,
                  o_ref, tail_ref, hc_ref, hs_ref, *, heads_per_step, hd):
    tt = x_ref.shape[0]

    @pl.when(pl.program_id(2) == 0)
    def _():
        tail_ref[...] = jnp.zeros_like(tail_ref)
        hc_ref[...] = jnp.zeros_like(hc_ref)

    xr = x_ref[...].astype(F32)
    tail = tail_ref[...]
    xc = cb_ref[...] + xr * cw_ref[CONV_WIDTH - 1:CONV_WIDTH, :]
    for s in range(1, CONV_WIDTH):
        k = CONV_WIDTH - 1 - s
        xc = xc + _shift_rows(xr, tail, s) * cw_ref[k:k + 1, :]
    tail_ref[...] = xr[tt - SUBLANES:, :]

    xcb = xc.astype(BF16)
    r_parts, i_parts = [], []
    for h in range(heads_per_step):
        cols = slice(h * hd, (h + 1) * hd)
        r_parts.append(_bdot(xcb[:, cols], wr_ref[h].astype(BF16)))
        i_parts.append(_bdot(xcb[:, cols], wi_ref[h].astype(BF16)))
    r = jax.nn.sigmoid(jnp.concatenate(r_parts, axis=1) + br_ref[...])
    ig = jax.nn.sigmoid(jnp.concatenate(i_parts, axis=1) + bi_ref[...])

    lam = lam_ref[...]
    softplus_neg_lam = jnp.maximum(-lam, 0.0) + jnp.log1p(jnp.exp(-jnp.abs(lam)))
    log_a = (-LRU_C * r) * softplus_neg_lam
    a = jnp.exp(log_a)
    th = jnp.tanh(log_a)
    mult = jnp.sqrt(-2.0 * th / (1.0 - th))
    b = mult * (ig * xc)

    rmod = lax.broadcasted_iota(I32, a.shape, 0) % SUBLANES
    for d in (1, 2, 4):
        keep = rmod >= d
        a_prev = jnp.where(keep, pltpu.roll(a, d, 0), 1.0)
        b_prev = jnp.where(keep, pltpu.roll(b, d, 0), 0.0)
        b = a * b_prev + b
        a = a * a_prev

    carry = hc_ref[...]
    for j in range(tt // SUBLANES):
        rows = slice(j * SUBLANES, (j + 1) * SUBLANES)
        hj = a[rows, :] * carry + b[rows, :]
        hs_ref[rows, :] = hj
        carry = jnp.broadcast_to(hj[SUBLANES - 1:SUBLANES, :], hj.shape)
    hc_ref[...] = carry

    o_ref[...] = (hs_ref[...] * y_ref[...].astype(F32)).astype(o_ref.dtype)


def _rglru(yx_bf, bsz, conv_w, conv_b, w_r, b_r, w_i, b_i, lam, tt, heads_per_step):
    t, w2 = yx_bf.shape
    w = w2 // 2
    hd = w // N_HEADS
    seq = t // bsz
    tt = min(tt, seq)
    hps = heads_per_step
    cg = hps * hd
    assert seq % tt == 0 and N_HEADS % hps == 0 and tt % SUBLANES == 0
    n_g, n_s = N_HEADS // hps, seq // tt
    row = lambda g, b, s: b * n_s + s
    vec = lambda g, b, s: (0, g)
    return pl.pallas_call(
        functools.partial(_rglru_kernel, heads_per_step=hps, hd=hd),
        grid=(n_g, bsz, n_s),
        in_specs=[pl.BlockSpec((tt, cg), lambda g, b, s: (row(g, b, s), g)),
                  pl.BlockSpec((tt, cg), lambda g, b, s: (row(g, b, s), n_g + g)),
                  pl.BlockSpec((CONV_WIDTH, cg), vec),
                  pl.BlockSpec((1, cg), vec),
                  pl.BlockSpec((hps, hd, hd), lambda g, b, s: (g, 0, 0)),
                  pl.BlockSpec((1, cg), vec),
                  pl.BlockSpec((hps, hd, hd), lambda g, b, s: (g, 0, 0)),
                  pl.BlockSpec((1, cg), vec),
                  pl.BlockSpec((1, cg), vec)],
        out_specs=pl.BlockSpec((tt, cg), lambda g, b, s: (row(g, b, s), g)),
        out_shape=jax.ShapeDtypeStruct((t, w), BF16),
        scratch_shapes=[pltpu.VMEM((SUBLANES, cg), F32),
                        pltpu.VMEM((SUBLANES, cg), F32),
                        pltpu.VMEM((tt, cg), F32)],
        compiler_params=_cparams(3),
        name="rglru",
    )(yx_bf, yx_bf, conv_w, conv_b.reshape(1, w), w_r, b_r.reshape(1, w),
      w_i, b_i.reshape(1, w), lam.reshape(1, w))


def _out_ln_kernel(x_ref, w_ref, h_ref, g_ref, b_ref, o_ref, *, alpha):
    y = alpha * h_ref[...] + _bdot(x_ref[...], w_ref[...])
    o_ref[...] = _layer_norm(y, g_ref[...], b_ref[...])


def _out_ln(x_bf, w_bf, h, ln_g, ln_b, alpha, tm):
    t, k = x_bf.shape
    d = w_bf.shape[1]
    tm = min(tm, t)
    assert t % tm == 0
    return pl.pallas_call(
        functools.partial(_out_ln_kernel, alpha=alpha),
        grid=(t // tm,),
        in_specs=[pl.BlockSpec((tm, k), lambda i: (i, 0)),
                  pl.BlockSpec((k, d), lambda i: (0, 0), pipeline_mode=pl.Buffered(1)),
                  pl.BlockSpec((tm, d), lambda i: (i, 0)),
                  pl.BlockSpec((1, d), lambda i: (0, 0)),
                  pl.BlockSpec((1, d), lambda i: (0, 0))],
        out_specs=pl.BlockSpec((tm, d), lambda i: (i, 0)),
        out_shape=jax.ShapeDtypeStruct((t, d), F32),
        compiler_params=_cparams(1),
        name="out_ln",
    )(x_bf, w_bf, h, ln_g.reshape(1, d), ln_b.reshape(1, d))


def _router_kernel(h_ref, wt_ref, rb_ref, e_ref, g_ref, p_ref, cnt_ref, carry_ref):
    n_e, tm = wt_ref.shape[0], h_ref.shape[0]

    @pl.when(pl.program_id(0) == 0)
    def _():
        carry_ref[...] = jnp.zeros_like(carry_ref)

    x = h_ref[...]
    xh = x.astype(BF16)
    xl = (x - xh.astype(F32)).astype(BF16)
    w = wt_ref[...]
    wh = w.astype(BF16)
    wl = (w - wh.astype(F32)).astype(BF16)
    dg = lambda a, b: lax.dot_general(a, b, NT_DIMS, preferred_element_type=F32)
    logits = dg(wh, xh) + dg(wh, xl) + dg(wl, xh) + rb_ref[...]

    eidx = lax.broadcasted_iota(I32, (n_e, tm), 0)
    work = logits
    tops, sel = [], []
    for _ in range(TOP_K):
        m = jnp.max(work, axis=0, keepdims=True)
        e = jnp.min(jnp.where(work == m, eidx, n_e), axis=0, keepdims=True)
        hit = eidx == e
        tops.append(m)
        sel.append(hit)
        e_ref[len(sel) - 1:len(sel), :] = e
        work = jnp.where(hit, -jnp.inf, work)

    ps = [jnp.exp(m - tops[0]) for m in tops]
    denom = ps[0] + ps[1] + ps[2] + ps[3]
    for k in range(TOP_K):
        g_ref[k:k + 1, :] = ps[k] / denom

    onehot = jnp.where(sel[0] | sel[1] | sel[2] | sel[3], 1.0, 0.0)
    before = (lax.broadcasted_iota(I32, (tm, tm), 0)
              < lax.broadcasted_iota(I32, (tm, tm), 1)).astype(BF16)
    rank = _bdot(onehot.astype(BF16), before) + carry_ref[:, 0:1]
    for k in range(TOP_K):
        pk = jnp.sum(jnp.where(sel[k], rank, 0.0), axis=0, keepdims=True)
        p_ref[k:k + 1, :] = pk.astype(I32)
    carry_ref[...] = carry_ref[...] + jnp.sum(onehot, axis=1, keepdims=True)
    cnt_ref[...] = carry_ref[...]


def _router(h, router_w, router_b, tm):
    t, d = h.shape
    n_e = router_w.shape[1]
    tm = min(tm, t)
    assert t % tm == 0
    tok = lambda i: (0, i)
    return pl.pallas_call(
        _router_kernel,
        grid=(t // tm,),
        in_specs=[pl.BlockSpec((tm, d), lambda i: (i, 0)),
                  pl.BlockSpec((n_e, d), lambda i: (0, 0)),
                  pl.BlockSpec((n_e, 1), lambda i: (0, 0))],
        out_specs=[pl.BlockSpec((TOP_K, tm), tok),
                   pl.BlockSpec((TOP_K, tm), tok),
                   pl.BlockSpec((TOP_K, tm), tok),
                   pl.BlockSpec((n_e, LANES), lambda i: (0, 0))],
        out_shape=[jax.ShapeDtypeStruct((TOP_K, t), I32),
                   jax.ShapeDtypeStruct((TOP_K, t), F32),
                   jax.ShapeDtypeStruct((TOP_K, t), I32),
                   jax.ShapeDtypeStruct((n_e, LANES), F32)],
        scratch_shapes=[pltpu.VMEM((n_e, LANES), F32)],
        compiler_params=_cparams(1),
        name="router",
    )(h, router_w.T, router_b.reshape(n_e, 1))


def _push_kernel(dest_ref, pad_lo_ref, pad_hi_ref, h_ref, z_ref, xs_ref, sem, zsem, *, tm, n_pad):
    i = pl.program_id(0)
    base = i * tm

    def row_copy(t, k):
        return pltpu.make_async_copy(h_ref.at[pl.ds(base + t, 1)],
                                     xs_ref.at[pl.ds(dest_ref[0, k, t], 1)], sem)

    def issue(t, c):
        for k in range(TOP_K):
            row_copy(t, k).start()
        return c

    lax.fori_loop(0, tm, issue, 0)

    def drain(t, c):
        for k in range(TOP_K):
            row_copy(t, k).wait()
        return c

    lax.fori_loop(0, tm, drain, 0)

    @pl.when(i == pl.num_programs(0) - 1)
    def _():
        def zero_copy(r):
            return pltpu.make_async_copy(z_ref.at[pl.ds(0, 1)], xs_ref.at[pl.ds(r, 1)], zsem)

        for e in range(n_pad):
            lo, hi = pad_lo_ref[e], pad_hi_ref[e]
            lax.fori_loop(lo, hi, lambda r, c: (zero_copy(r).start(), c)[1], 0)
            lax.fori_loop(lo, hi, lambda r, c: (zero_copy(r).wait(), c)[1], 0)


def _push(h, dest3, pad_lo, pad_hi, n_rows):
    t, d = h.shape
    n_steps, _, tm = dest3.shape
    zeros = jnp.zeros((SUBLANES, d), h.dtype)
    return pl.pallas_call(
        functools.partial(_push_kernel, tm=tm, n_pad=pad_lo.shape[0]),
        grid=(n_steps,),
        in_specs=[pl.BlockSpec((1, TOP_K, tm), lambda i: (i, 0, 0), memory_space=pltpu.SMEM),
                  pl.BlockSpec(memory_space=pltpu.SMEM),
                  pl.BlockSpec(memory_space=pltpu.SMEM),
                  pl.BlockSpec(memory_space=pl.ANY),
                  pl.BlockSpec(memory_space=pl.ANY)],
        out_specs=pl.BlockSpec(memory_space=pl.ANY),
        out_shape=jax.ShapeDtypeStruct((n_rows, d), h.dtype),
        scratch_shapes=[pltpu.SemaphoreType.DMA, pltpu.SemaphoreType.DMA],
        compiler_params=_cparams(1),
        name="push",
    )(dest3, pad_lo, pad_hi, h, zeros)


def _expert_kernel(te_ref, ts_ref, nu_ref, x_ref, wgu_ref, bgu_ref, wd_ref, bd_ref, o_ref, *, ff):
    used = pl.program_id(0) < nu_ref[0]

    @pl.when(used)
    def _():
        gu = _bdot(x_ref[...].astype(BF16), wgu_ref[...]) + bgu_ref[...]
        hg = jnp.minimum(gu[:, :ff], SWIGLU_LIMIT)
        hu = jnp.clip(gu[:, ff:], -SWIGLU_LIMIT, SWIGLU_LIMIT)
        hdn = hg * jax.nn.sigmoid(SWIGLU_ALPHA * hg) * (hu + 1.0)
        o_ref[...] = _bdot(hdn.astype(BF16), wd_ref[...]) + bd_ref[...]

    @pl.when(jnp.logical_not(used))
    def _():
        o_ref[...] = jnp.zeros_like(o_ref)


def _experts(xs, tile_e, tile_src, n_used, w_gu, b_gu, w_down, b_down, tr):
    n_rows, d = xs.shape
    n_e, _, ff2 = w_gu.shape
    ff = ff2 // 2
    n_tiles = n_rows // tr
    grid_spec = pltpu.PrefetchScalarGridSpec(
        num_scalar_prefetch=3,
        grid=(n_tiles,),
        in_specs=[pl.BlockSpec((tr, d), lambda i, te, ts, nu: (ts[i], 0)),
                  pl.BlockSpec((None, d, ff2), lambda i, te, ts, nu: (te[i], 0, 0)),
                  pl.BlockSpec((None, 1, ff2), lambda i, te, ts, nu: (te[i], 0, 0)),
                  pl.BlockSpec((None, ff, d), lambda i, te, ts, nu: (te[i], 0, 0)),
                  pl.BlockSpec((None, 1, d), lambda i, te, ts, nu: (te[i], 0, 0))],
        out_specs=pl.BlockSpec((tr, d), lambda i, te, ts, nu: (i, 0)),
    )
    return pl.pallas_call(
        functools.partial(_expert_kernel, ff=ff),
        grid_spec=grid_spec,
        out_shape=jax.ShapeDtypeStruct((n_rows, d), F32),
        compiler_params=_cparams(1),
        name="experts",
    )(tile_e, tile_src, n_used, xs, w_gu, b_gu.reshape(n_e, 1, ff2), w_down,
      b_down.reshape(n_e, 1, d))


def _combine_kernel(dcur_ref, dnext_ref, ys_ref, h_ref, gate_ref, g_ref, b_ref,
                    o_ref, obf_ref, buf_ref, sem, *, alpha, tm):
    i = pl.program_id(0)
    n = pl.num_programs(0)
    slot = i % 2

    def row_copy(dref, s, t, k):
        return pltpu.make_async_copy(ys_ref.at[pl.ds(dref[0, k, t], 1)],
                                     buf_ref.at[s, k, pl.ds(t, 1)], sem.at[s])

    def issue(dref, s):
        def body(t, c):
            for k in range(TOP_K):
                row_copy(dref, s, t, k).start()
            return c
        lax.fori_loop(0, tm, body, 0)

    @pl.when(i == 0)
    def _():
        issue(dcur_ref, 0)

    @pl.when(i + 1 < n)
    def _():
        issue(dnext_ref, 1 - slot)

    def drain(t, c):
        for k in range(TOP_K):
            row_copy(dcur_ref, slot, t, k).wait()
        return c

    lax.fori_loop(0, tm, drain, 0)

    gate = gate_ref[...]
    y = alpha * h_ref[...]
    for k in range(TOP_K):
        y = y + gate[:, k:k + 1] * buf_ref[slot, k]
    out = _layer_norm(y, g_ref[...], b_ref[...])
    o_ref[...] = out
    obf_ref[...] = out.astype(obf_ref.dtype)


def _combine(ys, dest3, h, gate_t, ln_g, ln_b, alpha):
    t, d = h.shape
    n_steps, _, tm = dest3.shape
    smem = lambda f: pl.BlockSpec((1, TOP_K, tm), f, memory_space=pltpu.SMEM)
    return pl.pallas_call(
        functools.partial(_combine_kernel, alpha=alpha, tm=tm),
        grid=(n_steps,),
        in_specs=[smem(lambda i: (i, 0, 0)),
                  smem(lambda i: (jnp.minimum(i + 1, n_steps - 1), 0, 0)),
                  pl.BlockSpec(memory_space=pl.ANY),
                  pl.BlockSpec((tm, d), lambda i: (i, 0)),
                  pl.BlockSpec((tm, TOP_K), lambda i: (i, 0)),
                  pl.BlockSpec((1, d), lambda i: (0, 0)),
                  pl.BlockSpec((1, d), lambda i: (0, 0))],
        out_specs=[pl.BlockSpec((tm, d), lambda i: (i, 0)),
                   pl.BlockSpec((tm, d), lambda i: (i, 0))],
        out_shape=[jax.ShapeDtypeStruct((t, d), F32),
                   jax.ShapeDtypeStruct((t, d), BF16)],
        scratch_shapes=[pltpu.VMEM((2, TOP_K, tm, d), F32),
                        pltpu.SemaphoreType.DMA((2,))],
        compiler_params=_cparams(1),
        name="combine",
    )(dest3, dest3, ys, h, gate_t, ln_g.reshape(1, d), ln_b.reshape(1, d))


PROJ_TM, PROJ_TN = 1024, 1024
SGU_ROWS = 512
RGLRU_TT, RGLRU_HEADS = 512, 2
OUT_TM = 128
ROUTER_TM = 512
PUSH_TM = 512
EXPERT_ROWS = 256
COMBINE_TM = 128


def _blocked_index(a, tm):
    k, t = a.shape
    return a.reshape(k, t // tm, tm).transpose(1, 0, 2)


def _moe(h, alpha, router_w, router_b, w_gu, b_gu, w_down, b_down, ln_g, ln_b):
    t, d = h.shape
    n_e = router_w.shape[1]
    tr = EXPERT_ROWS
    top_e, gate, pos, cnt = _router(h, router_w, router_b, ROUTER_TM)

    counts = cnt[:, 0].astype(I32)
    padded = (counts + tr - 1) // tr * tr
    pad_end = jnp.cumsum(padded)
    pad_start = pad_end - padded
    dest = pad_start[top_e] + pos
    n_tiles = (t * TOP_K + n_e * (tr - 1)) // tr + 1
    n_used = (pad_end[-1] // tr).astype(I32)
    tile_id = jnp.arange(n_tiles, dtype=I32)
    tile_e = jnp.minimum(jnp.searchsorted(pad_end, tile_id * tr, side="right"), n_e - 1).astype(I32)
    tile_src = jnp.minimum(tile_id, n_used - 1)

    n_rows = n_tiles * tr
    pad_lo = jnp.concatenate([pad_start + counts, pad_end[-1:]]).astype(I32)
    pad_hi = jnp.concatenate([pad_end, jnp.full((1,), n_rows)]).astype(I32)
    xs = _push(h, _blocked_index(dest, min(PUSH_TM, t)), pad_lo, pad_hi, n_rows)
    ys = _experts(xs, tile_e, tile_src, n_used.reshape(1), w_gu, b_gu, w_down, b_down, tr)
    return _combine(ys, _blocked_index(dest, min(COMBINE_TM, t)), h, gate.T, ln_g, ln_b, alpha)


def kernel(x, a_w_in, a_ln_g, a_ln_b, a_w_s, a_b_s, a_w_out, b_w_in, b_conv_w, b_conv_b, b_w_r, b_b_r, b_w_i, b_b_i, b_lambda, b_w_out, ln1_g, ln1_b, ln2_g, ln2_b, router_w, router_b, ex_w_gu, ex_b_gu, ex_w_down, ex_b_down):
    bsz, seq, d = x.shape
    depth = ln1_g.shape[0]
    alpha = (2 * depth) ** 0.25
    h = x.reshape(bsz * seq, d)
    h_bf = h.astype(BF16)
    for layer in range(depth):
        j = layer // 2
        if layer % 2 == 0:
            z = _proj(h_bf, a_w_in[j].astype(BF16), 2 * d, PROJ_TM, PROJ_TN)
            mixed = _sgu(z, a_ln_g[j], a_ln_b[j], a_w_s[j], a_b_s[j], SGU_ROWS)
            w_out = a_w_out[j]
        else:
            yx = _proj(h_bf, b_w_in[j].astype(BF16), d, PROJ_TM, PROJ_TN)
            mixed = _rglru(yx, bsz, b_conv_w[j], b_conv_b[j], b_w_r[j], b_b_r[j],
                           b_w_i[j], b_b_i[j], b_lambda[j], RGLRU_TT, RGLRU_HEADS)
            w_out = b_w_out[j]
        h1 = _out_ln(mixed, w_out.astype(BF16), h, ln1_g[layer], ln1_b[layer], alpha, OUT_TM)
        h, h_bf = _moe(h1, alpha, router_w[layer], router_b[layer],
                       ex_w_gu[layer].astype(BF16), ex_b_gu[layer],
                       ex_w_down[layer].astype(BF16), ex_b_down[layer],
                       ln2_g[layer], ln2_b[layer])
    return h.reshape(bsz, seq, d)
```

```python
import functools

import jax
import jax.numpy as jnp
from jax import lax
from jax.experimental import pallas as pl
from jax.experimental.pallas import tpu as pltpu

F32 = jnp.float32
BF16 = jnp.bfloat16
I32 = jnp.int32
U32 = jnp.uint32

CHUNK = 128
N_HEADS = 16
CONV_WIDTH = 4
LRU_C = 8.0
TOP_K = 4
SWIGLU_LIMIT = 7.0
SWIGLU_ALPHA = 1.702
LN_EPS = 1e-5

LANES = 128
SUBLANES = 8
VMEM_LIMIT_BYTES = 60000 * 1024

NT_DIMS = (((1,), (1,)), ((), ()))


def _cparams(n_axes, vmem=VMEM_LIMIT_BYTES):
    return pltpu.CompilerParams(
        dimension_semantics=("arbitrary",) * n_axes, vmem_limit_bytes=vmem)


def _gelu(x):
    return 0.5 * x * (1.0 + jnp.tanh(0.7978845608028654 * (x + 0.044715 * (x * x * x))))


def _layer_norm(x, g, b):
    mu = jnp.mean(x, axis=-1, keepdims=True)
    xc = x - mu
    var = jnp.mean(xc * xc, axis=-1, keepdims=True)
    return xc * lax.rsqrt(var + LN_EPS) * g + b


def _bdot(a, b):
    return jnp.dot(a, b, preferred_element_type=F32)


def _pack_rows(x):
    n2 = x.shape[1] // 2
    hi = lax.bitcast_convert_type(x[:, :n2].astype(BF16).astype(F32), U32)
    lo = lax.bitcast_convert_type(x[:, n2:].astype(BF16).astype(F32), U32)
    return hi | (lo >> 16)


def _unpack_rows(p):
    hi = lax.bitcast_convert_type(p & jnp.uint32(0xFFFF0000), F32)
    lo = lax.bitcast_convert_type(p << 16, F32)
    return hi, lo


def _proj_kernel(x_ref, w_ref, o_ref, *, gelu_panels):
    acc = _bdot(x_ref[...], w_ref[...])
    j = pl.program_id(0)

    @pl.when(j < gelu_panels)
    def _():
        o_ref[...] = _gelu(acc).astype(o_ref.dtype)

    @pl.when(j >= gelu_panels)
    def _():
        o_ref[...] = acc.astype(o_ref.dtype)


def _proj(x_bf, w_bf, n_gelu, tm, tn):
    t, k = x_bf.shape
    n = w_bf.shape[1]
    tm, tn = min(tm, t), min(tn, n)
    assert t % tm == 0 and n % tn == 0 and n_gelu % tn == 0
    return pl.pallas_call(
        functools.partial(_proj_kernel, gelu_panels=n_gelu // tn),
        grid=(n // tn, t // tm),
        in_specs=[pl.BlockSpec((tm, k), lambda j, i: (i, 0)),
                  pl.BlockSpec((k, tn), lambda j, i: (0, j))],
        out_specs=pl.BlockSpec((tm, tn), lambda j, i: (i, j)),
        out_shape=jax.ShapeDtypeStruct((t, n), BF16),
        compiler_params=_cparams(2),
        name="proj",
    )(x_bf, w_bf)


def _sgu_kernel(u_ref, v_ref, g_ref, b_ref, ws_ref, bs_ref, o_ref, *, n_chunks, hd):
    row = lax.broadcasted_iota(I32, (CHUNK, CHUNK), 0)
    col = lax.broadcasted_iota(I32, (CHUNK, CHUNK), 1)
    causal = row >= col
    for c in range(n_chunks):
        rows = slice(c * CHUNK, (c + 1) * CHUNK)
        vn = _layer_norm(v_ref[rows, :].astype(F32), g_ref[...], b_ref[...]).astype(BF16)
        for h in range(N_HEADS):
            cols = slice(h * hd, (h + 1) * hd)
            wm = jnp.where(causal, ws_ref[h], 0.0).astype(BF16)
            sv = _bdot(wm, vn[:, cols]) + bs_ref[:, h:h + 1]
            o_ref[rows, cols] = (u_ref[rows, cols].astype(F32) * sv).astype(o_ref.dtype)


def _sgu(z_bf, ln_g, ln_b, w_s, b_s, rows_per_step):
    t, w2 = z_bf.shape
    w = w2 // 2
    hd = w // N_HEADS
    r = min(rows_per_step, t)
    assert t % r == 0 and r % CHUNK == 0
    return pl.pallas_call(
        functools.partial(_sgu_kernel, n_chunks=r // CHUNK, hd=hd),
        grid=(t // r,),
        in_specs=[pl.BlockSpec((r, w), lambda i: (i, 0)),
                  pl.BlockSpec((r, w), lambda i: (i, 1)),
                  pl.BlockSpec((1, w), lambda i: (0, 0)),
                  pl.BlockSpec((1, w), lambda i: (0, 0)),
                  pl.BlockSpec((N_HEADS, CHUNK, CHUNK), lambda i: (0, 0, 0)),
                  pl.BlockSpec((CHUNK, N_HEADS), lambda i: (0, 0))],
        out_specs=pl.BlockSpec((r, w), lambda i: (i, 0)),
        out_shape=jax.ShapeDtypeStruct((t, w), BF16),
        compiler_params=_cparams(1),
        name="sgu",
    )(z_bf, z_bf, ln_g.reshape(1, w), ln_b.reshape(1, w), w_s, b_s.T)


def _shift_rows(x, tail, s):
    rolled = pltpu.roll(x, s, 0)
    tail_rolled = pltpu.roll(tail, s, 0)
    first = jnp.where(lax.broadcasted_iota(I32, tail.shape, 0) < s, tail_rolled, rolled[:SUBLANES])
    return jnp.concatenate([first, rolled[SUBLANES:]], axis=0)


def _rglru_kernel(y_ref, x_ref, cw_ref, cb_ref, wr_ref, br_ref, wi_ref, bi_ref, lam_ref,
                  o_ref, tail_ref, hc_ref, hs_ref, *, heads_per_step, hd):
    tt = x_ref.shape[0]

    @pl.when(pl.program_id(2) == 0)
    def _():
        tail_ref[...] = jnp.zeros_like(tail_ref)
        hc_ref[...] = jnp.zeros_like(hc_ref)

    xr = x_ref[...].astype(F32)
    tail = tail_ref[...]
    xc = cb_ref[...] + xr * cw_ref[CONV_WIDTH - 1:CONV_WIDTH, :]
    for s in range(1, CONV_WIDTH):
        k = CONV_WIDTH - 1 - s
        xc = xc + _shift_rows(xr, tail, s) * cw_ref[k:k + 1, :]
    tail_ref[...] = xr[tt - SUBLANES:, :]

    xcb = xc.astype(BF16)
    r_parts, i_parts = [], []
    for h in range(heads_per_step):
        cols = slice(h * hd, (h + 1) * hd)
        r_parts.append(_bdot(xcb[:, cols], wr_ref[h].astype(BF16)))
        i_parts.append(_bdot(xcb[:, cols], wi_ref[h].astype(BF16)))
    r = jax.nn.sigmoid(jnp.concatenate(r_parts, axis=1) + br_ref[...])
    ig = jax.nn.sigmoid(jnp.concatenate(i_parts, axis=1) + bi_ref[...])

    lam = lam_ref[...]
    softplus_neg_lam = jnp.maximum(-lam, 0.0) + jnp.log1p(jnp.exp(-jnp.abs(lam)))
    log_a = (-LRU_C * r) * softplus_neg_lam
    a = jnp.exp(log_a)
    th = jnp.tanh(log_a)
    mult = jnp.sqrt(-2.0 * th / (1.0 - th))
    b = mult * (ig * xc)

    rmod = lax.broadcasted_iota(I32, a.shape, 0) % SUBLANES
    for d in (1, 2, 4):
        keep = rmod >= d
        a_prev = jnp.where(keep, pltpu.roll(a, d, 0), 1.0)
        b_prev = jnp.where(keep, pltpu.roll(b, d, 0), 0.0)
        b = a * b_prev + b
        a = a * a_prev

    carry = hc_ref[...]
    for j in range(tt // SUBLANES):
        rows = slice(j * SUBLANES, (j + 1) * SUBLANES)
        hj = a[rows, :] * carry + b[rows, :]
        hs_ref[rows, :] = hj
        carry = jnp.broadcast_to(hj[SUBLANES - 1:SUBLANES, :], hj.shape)
    hc_ref[...] = carry

    o_ref[...] = (hs_ref[...] * y_ref[...].astype(F32)).astype(o_ref.dtype)


def _rglru(yx_bf, bsz, conv_w, conv_b, w_r, b_r, w_i, b_i, lam, tt, heads_per_step):
    t, w2 = yx_bf.shape
    w = w2 // 2
    hd = w // N_HEADS
    seq = t // bsz
    tt = min(tt, seq)
    hps = heads_per_step
    cg = hps * hd
    assert seq % tt == 0 and N_HEADS % hps == 0 and tt % SUBLANES == 0
    n_g, n_s = N_HEADS // hps, seq // tt
    row = lambda g, b, s: b * n_s + s
    vec = lambda g, b, s: (0, g)
    return pl.pallas_call(
        functools.partial(_rglru_kernel, heads_per_step=hps, hd=hd),
        grid=(n_g, bsz, n_s),
        in_specs=[pl.BlockSpec((tt, cg), lambda g, b, s: (row(g, b, s), g)),
                  pl.BlockSpec((tt, cg), lambda g, b, s: (row(g, b, s), n_g + g)),
                  pl.BlockSpec((CONV_WIDTH, cg), vec),
                  pl.BlockSpec((1, cg), vec),
                  pl.BlockSpec((hps, hd, hd), lambda g, b, s: (g, 0, 0)),
                  pl.BlockSpec((1, cg), vec),
                  pl.BlockSpec((hps, hd, hd), lambda g, b, s: (g, 0, 0)),
                  pl.BlockSpec((1, cg), vec),
                  pl.BlockSpec((1, cg), vec)],
        out_specs=pl.BlockSpec((tt, cg), lambda g, b, s: (row(g, b, s), g)),
        out_shape=jax.ShapeDtypeStruct((t, w), BF16),
        scratch_shapes=[pltpu.VMEM((SUBLANES, cg), F32),
                        pltpu.VMEM((SUBLANES, cg), F32),
                        pltpu.VMEM((tt, cg), F32)],
        compiler_params=_cparams(3),
        name="rglru",
    )(yx_bf, yx_bf, conv_w, conv_b.reshape(1, w), w_r, b_r.reshape(1, w),
      w_i, b_i.reshape(1, w), lam.reshape(1, w))


def _out_ln_kernel(x_ref, w_ref, h_ref, g_ref, b_ref, o_ref, *, alpha):
    y = alpha * h_ref[...] + _bdot(x_ref[...], w_ref[...])
    o_ref[...] = _layer_norm(y, g_ref[...], b_ref[...])


def _out_ln(x_bf, w_bf, h, ln_g, ln_b, alpha, tm):
    t, k = x_bf.shape
    d = w_bf.shape[1]
    tm = min(tm, t)
    assert t % tm == 0
    return pl.pallas_call(
        functools.partial(_out_ln_kernel, alpha=alpha),
        grid=(t // tm,),
        in_specs=[pl.BlockSpec((tm, k), lambda i: (i, 0)),
                  pl.BlockSpec((k, d), lambda i: (0, 0), pipeline_mode=pl.Buffered(1)),
                  pl.BlockSpec((tm, d), lambda i: (i, 0)),
                  pl.BlockSpec((1, d), lambda i: (0, 0)),
                  pl.BlockSpec((1, d), lambda i: (0, 0))],
        out_specs=pl.BlockSpec((tm, d), lambda i: (i, 0)),
        out_shape=jax.ShapeDtypeStruct((t, d), F32),
        compiler_params=_cparams(1),
        name="out_ln",
    )(x_bf, w_bf, h, ln_g.reshape(1, d), ln_b.reshape(1, d))


def _router_kernel(h_ref, wt_ref, rb_ref, e_ref, g_ref, p_ref, cnt_ref, carry_ref):
    n_e, tm = wt_ref.shape[0], h_ref.shape[0]

    @pl.when(pl.program_id(0) == 0)
    def _():
        carry_ref[...] = jnp.zeros_like(carry_ref)

    x = h_ref[...]
    xh = x.astype(BF16)
    xl = (x - xh.astype(F32)).astype(BF16)
    w = wt_ref[...]
    wh = w.astype(BF16)
    wl = (w - wh.astype(F32)).astype(BF16)
    dg = lambda a, b: lax.dot_general(a, b, NT_DIMS, preferred_element_type=F32)
    logits = dg(wh, xh) + dg(wh, xl) + dg(wl, xh) + rb_ref[...]

    eidx = lax.broadcasted_iota(I32, (n_e, tm), 0)
    work = logits
    tops, sel = [], []
    for _ in range(TOP_K):
        m = jnp.max(work, axis=0, keepdims=True)
        e = jnp.min(jnp.where(work == m, eidx, n_e), axis=0, keepdims=True)
        hit = eidx == e
        tops.append(m)
        sel.append(hit)
        e_ref[len(sel) - 1:len(sel), :] = e
        work = jnp.where(hit, -jnp.inf, work)

    ps = [jnp.exp(m - tops[0]) for m in tops]
    denom = ps[0] + ps[1] + ps[2] + ps[3]
    for k in range(TOP_K):
        g_ref[k:k + 1, :] = ps[k] / denom

    onehot = jnp.where(sel[0] | sel[1] | sel[2] | sel[3], 1.0, 0.0)
    before = (lax.broadcasted_iota(I32, (tm, tm), 0)
              < lax.broadcasted_iota(I32, (tm, tm), 1)).astype(BF16)
    rank = _bdot(onehot.astype(BF16), before) + carry_ref[:, 0:1]
    for k in range(TOP_K):
        pk = jnp.sum(jnp.where(sel[k], rank, 0.0), axis=0, keepdims=True)
        p_ref[k:k + 1, :] = pk.astype(I32)
    carry_ref[...] = carry_ref[...] + jnp.sum(onehot, axis=1, keepdims=True)
    cnt_ref[...] = carry_ref[...]


def _router(h, router_w, router_b, tm):
    t, d = h.shape
    n_e = router_w.shape[1]
    tm = min(tm, t)
    assert t % tm == 0
    tok = lambda i: (0, i)
    return pl.pallas_call(
        _router_kernel,
        grid=(t // tm,),
        in_specs=[pl.BlockSpec((tm, d), lambda i: (i, 0)),
                  pl.BlockSpec((n_e, d), lambda i: (0, 0)),
                  pl.BlockSpec((n_e, 1), lambda i: (0, 0))],
        out_specs=[pl.BlockSpec((TOP_K, tm), tok),
                   pl.BlockSpec((TOP_K, tm), tok),
                   pl.BlockSpec((TOP_K, tm), tok),
                   pl.BlockSpec((n_e, LANES), lambda i: (0, 0))],
        out_shape=[jax.ShapeDtypeStruct((TOP_K, t), I32),
                   jax.ShapeDtypeStruct((TOP_K, t), F32),
                   jax.ShapeDtypeStruct((TOP_K, t), I32),
                   jax.ShapeDtypeStruct((n_e, LANES), F32)],
        scratch_shapes=[pltpu.VMEM((n_e, LANES), F32)],
        compiler_params=_cparams(1),
        name="router",
    )(h, router_w.T, router_b.reshape(n_e, 1))


def _push_kernel(dest_ref, pad_lo_ref, pad_hi_ref, h_ref, xs_ref, pk_ref, z_ref, sem, zsem,
                 *, tm, n_pad):
    i = pl.program_id(0)
    pk_ref[...] = _pack_rows(h_ref[...])

    def row_copy(t, k):
        return pltpu.make_async_copy(pk_ref.at[pl.ds(t, 1)],
                                     xs_ref.at[pl.ds(dest_ref[0, k, t], 1)], sem)

    def issue(t, c):
        for k in range(TOP_K):
            row_copy(t, k).start()
        return c

    lax.fori_loop(0, tm, issue, 0)

    def drain(t, c):
        for k in range(TOP_K):
            row_copy(t, k).wait()
        return c

    lax.fori_loop(0, tm, drain, 0)

    @pl.when(i == pl.num_programs(0) - 1)
    def _():
        z_ref[...] = jnp.zeros_like(z_ref)

        def zero_copy(r):
            return pltpu.make_async_copy(z_ref.at[pl.ds(0, 1)], xs_ref.at[pl.ds(r, 1)], zsem)

        for e in range(n_pad):
            lo, hi = pad_lo_ref[e], pad_hi_ref[e]
            lax.fori_loop(lo, hi, lambda r, c: (zero_copy(r).start(), c)[1], 0)
            lax.fori_loop(lo, hi, lambda r, c: (zero_copy(r).wait(), c)[1], 0)


def _push(h, dest3, pad_lo, pad_hi, n_rows):
    t, d = h.shape
    n_steps, _, tm = dest3.shape
    return pl.pallas_call(
        functools.partial(_push_kernel, tm=tm, n_pad=pad_lo.shape[0]),
        grid=(n_steps,),
        in_specs=[pl.BlockSpec((1, TOP_K, tm), lambda i: (i, 0, 0), memory_space=pltpu.SMEM),
                  pl.BlockSpec(memory_space=pltpu.SMEM),
                  pl.BlockSpec(memory_space=pltpu.SMEM),
                  pl.BlockSpec((tm, d), lambda i: (i, 0))],
        out_specs=pl.BlockSpec(memory_space=pl.ANY),
        out_shape=jax.ShapeDtypeStruct((n_rows, d // 2), U32),
        scratch_shapes=[pltpu.VMEM((tm, d // 2), U32),
                        pltpu.VMEM((SUBLANES, d // 2), U32),
                        pltpu.SemaphoreType.DMA, pltpu.SemaphoreType.DMA],
        compiler_params=_cparams(1),
        name="push",
    )(dest3, pad_lo, pad_hi, h)


def _expert_kernel(te_ref, ts_ref, nu_ref, x_ref, wgu_ref, bgu_ref, wd_ref, bd_ref, o_ref, *, ff):
    used = pl.program_id(0) < nu_ref[0]

    @pl.when(used)
    def _():
        x_hi, x_lo = _unpack_rows(x_ref[...])
        d2 = x_hi.shape[1]
        gu = (_bdot(x_hi.astype(BF16), wgu_ref[:d2, :]) + _bdot(x_lo.astype(BF16), wgu_ref[d2:, :])
              + bgu_ref[...])
        hg = jnp.minimum(gu[:, :ff], SWIGLU_LIMIT)
        hu = jnp.clip(gu[:, ff:], -SWIGLU_LIMIT, SWIGLU_LIMIT)
        hdn = hg * jax.nn.sigmoid(SWIGLU_ALPHA * hg) * (hu + 1.0)
        o_ref[...] = _pack_rows(_bdot(hdn.astype(BF16), wd_ref[...]) + bd_ref[...])

    @pl.when(jnp.logical_not(used))
    def _():
        o_ref[...] = jnp.zeros_like(o_ref)


def _experts(xs, tile_e, tile_src, n_used, w_gu, b_gu, w_down, b_down, tr):
    n_rows, d2 = xs.shape
    n_e, d, ff2 = w_gu.shape
    ff = ff2 // 2
    n_tiles = n_rows // tr
    grid_spec = pltpu.PrefetchScalarGridSpec(
        num_scalar_prefetch=3,
        grid=(n_tiles,),
        in_specs=[pl.BlockSpec((tr, d2), lambda i, te, ts, nu: (ts[i], 0)),
                  pl.BlockSpec((None, d, ff2), lambda i, te, ts, nu: (te[i], 0, 0)),
                  pl.BlockSpec((None, 1, ff2), lambda i, te, ts, nu: (te[i], 0, 0)),
                  pl.BlockSpec((None, ff, d), lambda i, te, ts, nu: (te[i], 0, 0)),
                  pl.BlockSpec((None, 1, d), lambda i, te, ts, nu: (te[i], 0, 0))],
        out_specs=pl.BlockSpec((tr, d2), lambda i, te, ts, nu: (i, 0)),
    )
    return pl.pallas_call(
        functools.partial(_expert_kernel, ff=ff),
        grid_spec=grid_spec,
        out_shape=jax.ShapeDtypeStruct((n_rows, d2), U32),
        compiler_params=_cparams(1),
        name="experts",
    )(tile_e, tile_src, n_used, xs, w_gu, b_gu.reshape(n_e, 1, ff2), w_down,
      b_down.reshape(n_e, 1, d))


def _combine_kernel(dcur_ref, dnext_ref, ys_ref, h_ref, gate_ref, g_ref, b_ref,
                    o_ref, obf_ref, buf_ref, sem, *, alpha, tm):
    i = pl.program_id(0)
    n = pl.num_programs(0)
    slot = i % 2

    def row_copy(dref, s, t, k):
        return pltpu.make_async_copy(ys_ref.at[pl.ds(dref[0, k, t], 1)],
                                     buf_ref.at[s, k, pl.ds(t, 1)], sem.at[s])

    def issue(dref, s):
        def body(t, c):
            for k in range(TOP_K):
                row_copy(dref, s, t, k).start()
            return c
        lax.fori_loop(0, tm, body, 0)

    @pl.when(i == 0)
    def _():
        issue(dcur_ref, 0)

    @pl.when(i + 1 < n)
    def _():
        issue(dnext_ref, 1 - slot)

    def drain(t, c):
        for k in range(TOP_K):
            row_copy(dcur_ref, slot, t, k).wait()
        return c

    lax.fori_loop(0, tm, drain, 0)

    gate = gate_ref[...]
    d2 = buf_ref.shape[-1]
    y_hi = alpha * h_ref[:, :d2]
    y_lo = alpha * h_ref[:, d2:]
    for k in range(TOP_K):
        e_hi, e_lo = _unpack_rows(buf_ref[slot, k])
        y_hi = y_hi + gate[:, k:k + 1] * e_hi
        y_lo = y_lo + gate[:, k:k + 1] * e_lo
    out = _layer_norm(jnp.concatenate([y_hi, y_lo], axis=1), g_ref[...], b_ref[...])
    o_ref[...] = out
    obf_ref[...] = out.astype(obf_ref.dtype)


def _combine(ys, dest3, h, gate_t, ln_g, ln_b, alpha):
    t, d = h.shape
    n_steps, _, tm = dest3.shape
    smem = lambda f: pl.BlockSpec((1, TOP_K, tm), f, memory_space=pltpu.SMEM)
    return pl.pallas_call(
        functools.partial(_combine_kernel, alpha=alpha, tm=tm),
        grid=(n_steps,),
        in_specs=[smem(lambda i: (i, 0, 0)),
                  smem(lambda i: (jnp.minimum(i + 1, n_steps - 1), 0, 0)),
                  pl.BlockSpec(memory_space=pl.ANY),
                  pl.BlockSpec((tm, d), lambda i: (i, 0)),
                  pl.BlockSpec((tm, TOP_K), lambda i: (i, 0)),
                  pl.BlockSpec((1, d), lambda i: (0, 0)),
                  pl.BlockSpec((1, d), lambda i: (0, 0))],
        out_specs=[pl.BlockSpec((tm, d), lambda i: (i, 0)),
                   pl.BlockSpec((tm, d), lambda i: (i, 0))],
        out_shape=[jax.ShapeDtypeStruct((t, d), F32),
                   jax.ShapeDtypeStruct((t, d), BF16)],
        scratch_shapes=[pltpu.VMEM((2, TOP_K, tm, d // 2), U32),
                        pltpu.SemaphoreType.DMA((2,))],
        compiler_params=_cparams(1),
        name="combine",
    )(dest3, dest3, ys, h, gate_t, ln_g.reshape(1, d), ln_b.reshape(1, d))


PROJ_TM, PROJ_TN = 1024, 1024
SGU_ROWS = 512
RGLRU_TT, RGLRU_HEADS = 512, 2
OUT_TM = 128
ROUTER_TM = 512
PUSH_TM = 512
EXPERT_ROWS = 256
COMBINE_TM = 256


def _blocked_index(a, tm):
    k, t = a.shape
    return a.reshape(k, t // tm, tm).transpose(1, 0, 2)


def _moe(h, alpha, router_w, router_b, w_gu, b_gu, w_down, b_down, ln_g, ln_b):
    t, d = h.shape
    n_e = router_w.shape[1]
    tr = EXPERT_ROWS
    top_e, gate, pos, cnt = _router(h, router_w, router_b, ROUTER_TM)

    counts = cnt[:, 0].astype(I32)
    padded = (counts + tr - 1) // tr * tr
    pad_end = jnp.cumsum(padded)
    pad_start = pad_end - padded
    expert_id = jnp.arange(n_e, dtype=I32)[:, None, None]
    dest = pos + jnp.sum(jnp.where(top_e[None] == expert_id, pad_start[:, None, None], 0), axis=0)
    n_tiles = (t * TOP_K + n_e * (tr - 1)) // tr + 1
    n_used = (pad_end[-1] // tr).astype(I32)
    tile_id = jnp.arange(n_tiles, dtype=I32)
    tile_e = jnp.sum((pad_end[None, :] <= (tile_id * tr)[:, None]).astype(I32), axis=1)
    tile_e = jnp.minimum(tile_e, n_e - 1)
    tile_src = jnp.minimum(tile_id, n_used - 1)

    n_rows = n_tiles * tr
    pad_lo = jnp.concatenate([pad_start + counts, pad_end[-1:]]).astype(I32)
    pad_hi = jnp.concatenate([pad_end, jnp.full((1,), n_rows)]).astype(I32)
    xs = _push(h, _blocked_index(dest, min(PUSH_TM, t)), pad_lo, pad_hi, n_rows)
    ys = _experts(xs, tile_e, tile_src, n_used.reshape(1), w_gu, b_gu, w_down, b_down, tr)
    return _combine(ys, _blocked_index(dest, min(COMBINE_TM, t)), h, gate.T, ln_g, ln_b, alpha)


def kernel(x, a_w_in, a_ln_g, a_ln_b, a_w_s, a_b_s, a_w_out, b_w_in, b_conv_w, b_conv_b, b_w_r, b_b_r, b_w_i, b_b_i, b_lambda, b_w_out, ln1_g, ln1_b, ln2_g, ln2_b, router_w, router_b, ex_w_gu, ex_b_gu, ex_w_down, ex_b_down):
    bsz, seq, d = x.shape
    depth = ln1_g.shape[0]
    alpha = (2 * depth) ** 0.25
    h = x.reshape(bsz * seq, d)
    h_bf = h.astype(BF16)
    for layer in range(depth):
        j = layer // 2
        if layer % 2 == 0:
            z = _proj(h_bf, a_w_in[j].astype(BF16), 2 * d, PROJ_TM, PROJ_TN)
            mixed = _sgu(z, a_ln_g[j], a_ln_b[j], a_w_s[j], a_b_s[j], SGU_ROWS)
            w_out = a_w_out[j]
        else:
            yx = _proj(h_bf, b_w_in[j].astype(BF16), d, PROJ_TM, PROJ_TN)
            mixed = _rglru(yx, bsz, b_conv_w[j], b_conv_b[j], b_w_r[j], b_b_r[j],
                           b_w_i[j], b_b_i[j], b_lambda[j], RGLRU_TT, RGLRU_HEADS)
            w_out = b_w_out[j]
        h1 = _out_ln(mixed, w_out.astype(BF16), h, ln1_g[layer], ln1_b[layer], alpha, OUT_TM)
        h, h_bf = _moe(h1, alpha, router_w[layer], router_b[layer],
                       ex_w_gu[layer].astype(BF16), ex_b_gu[layer],
                       ex_w_down[layer].astype(BF16), ex_b_down[layer],
                       ln2_g[layer], ln2_b[layer])
    return h.reshape(bsz, seq, d)
```

```python
import functools

import jax
import jax.numpy as jnp
from jax import lax
from jax.experimental import pallas as pl
from jax.experimental.pallas import tpu as pltpu

F32 = jnp.float32
BF16 = jnp.bfloat16
I32 = jnp.int32
U32 = jnp.uint32

CHUNK = 128
N_HEADS = 16
CONV_WIDTH = 4
LRU_C = 8.0
TOP_K = 4
SWIGLU_LIMIT = 7.0
SWIGLU_ALPHA = 1.702
LN_EPS = 1e-5

LANES = 128
SUBLANES = 8
VMEM_LIMIT_BYTES = 60000 * 1024

NT_DIMS = (((1,), (1,)), ((), ()))


def _cparams(n_axes, vmem=VMEM_LIMIT_BYTES):
    return pltpu.CompilerParams(
        dimension_semantics=("arbitrary",) * n_axes, vmem_limit_bytes=vmem)


def _gelu(x):
    return 0.5 * x * (1.0 + jnp.tanh(0.7978845608028654 * (x + 0.044715 * (x * x * x))))


def _layer_norm(x, g, b):
    mu = jnp.mean(x, axis=-1, keepdims=True)
    xc = x - mu
    var = jnp.mean(xc * xc, axis=-1, keepdims=True)
    return xc * lax.rsqrt(var + LN_EPS) * g + b


def _bdot(a, b):
    return jnp.dot(a, b, preferred_element_type=F32)


def _pack_rows(x):
    n2 = x.shape[1] // 2
    hi = lax.bitcast_convert_type(x[:, :n2].astype(BF16).astype(F32), U32)
    lo = lax.bitcast_convert_type(x[:, n2:].astype(BF16).astype(F32), U32)
    return hi | (lo >> 16)


def _unpack_rows(p):
    hi = lax.bitcast_convert_type(p & jnp.uint32(0xFFFF0000), F32)
    lo = lax.bitcast_convert_type(p << 16, F32)
    return hi, lo


def _proj_kernel(x_ref, w_ref, o_ref, wbf_ref, *, gelu, n_chunks):
    @pl.when(pl.program_id(1) == 0)
    def _():
        wbf_ref[...] = w_ref[...].astype(BF16)

    rows_per_chunk = x_ref.shape[0] // n_chunks
    for c in range(n_chunks):
        rows = slice(c * rows_per_chunk, (c + 1) * rows_per_chunk)
        acc = _bdot(x_ref[rows, :], wbf_ref[...])
        o_ref[rows, :] = (_gelu(acc) if gelu else acc).astype(o_ref.dtype)


def _proj(x_bf, w, layer, col0, n, gelu, tm, tn):
    t, k = x_bf.shape
    tm, tn = min(tm, t), min(tn, n)
    assert t % tm == 0 and n % tn == 0 and col0 % tn == 0
    return pl.pallas_call(
        functools.partial(_proj_kernel, gelu=gelu, n_chunks=PROJ_CHUNKS),
        grid=(n // tn, t // tm),
        in_specs=[pl.BlockSpec((tm, k), lambda j, i: (i, 0)),
                  pl.BlockSpec((None, k, tn), lambda j, i: (layer, 0, col0 // tn + j),
                               pipeline_mode=pl.Buffered(1))],
        out_specs=pl.BlockSpec((tm, tn), lambda j, i: (i, j)),
        out_shape=jax.ShapeDtypeStruct((t, n), BF16),
        scratch_shapes=[pltpu.VMEM((k, tn), BF16)],
        compiler_params=_cparams(2),
        name="proj",
    )(x_bf, w)


def _sgu_kernel(u_ref, v_ref, g_ref, b_ref, ws_ref, bs_ref, o_ref, *, n_chunks, hd):
    row = lax.broadcasted_iota(I32, (CHUNK, CHUNK), 0)
    col = lax.broadcasted_iota(I32, (CHUNK, CHUNK), 1)
    causal = row >= col
    for c in range(n_chunks):
        rows = slice(c * CHUNK, (c + 1) * CHUNK)
        vn = _layer_norm(v_ref[rows, :].astype(F32), g_ref[...], b_ref[...]).astype(BF16)
        for h in range(N_HEADS):
            cols = slice(h * hd, (h + 1) * hd)
            wm = jnp.where(causal, ws_ref[h], 0.0).astype(BF16)
            sv = _bdot(wm, vn[:, cols]) + bs_ref[:, h:h + 1]
            o_ref[rows, cols] = (u_ref[rows, cols].astype(F32) * sv).astype(o_ref.dtype)


def _sgu(z_bf, ln_g, ln_b, w_s, b_s, rows_per_step):
    t, w2 = z_bf.shape
    w = w2 // 2
    hd = w // N_HEADS
    r = min(rows_per_step, t)
    assert t % r == 0 and r % CHUNK == 0
    return pl.pallas_call(
        functools.partial(_sgu_kernel, n_chunks=r // CHUNK, hd=hd),
        grid=(t // r,),
        in_specs=[pl.BlockSpec((r, w), lambda i: (i, 0)),
                  pl.BlockSpec((r, w), lambda i: (i, 1)),
                  pl.BlockSpec((1, w), lambda i: (0, 0)),
                  pl.BlockSpec((1, w), lambda i: (0, 0)),
                  pl.BlockSpec((N_HEADS, CHUNK, CHUNK), lambda i: (0, 0, 0)),
                  pl.BlockSpec((CHUNK, N_HEADS), lambda i: (0, 0))],
        out_specs=pl.BlockSpec((r, w), lambda i: (i, 0)),
        out_shape=jax.ShapeDtypeStruct((t, w), BF16),
        compiler_params=_cparams(1),
        name="sgu",
    )(z_bf, z_bf, ln_g.reshape(1, w), ln_b.reshape(1, w), w_s, b_s.T)


def _rglru_kernel(y_ref, x_ref, cw_ref, cb_ref, wr_ref, br_ref, wi_ref, bi_ref, lam_ref,
                  o_ref, tail_ref, hc_ref, hs_ref, *, heads_per_step, hd):
    tt, cg = x_ref.shape
    n8 = tt // SUBLANES

    @pl.when(pl.program_id(2) == 0)
    def _():
        tail_ref[...] = jnp.zeros_like(tail_ref)
        hc_ref[...] = jnp.zeros_like(hc_ref)

    sub = lax.broadcasted_iota(I32, (1, SUBLANES, 1), 1)
    xr = x_ref[...].astype(F32)
    x3 = xr.reshape(n8, SUBLANES, cg)
    tail3 = tail_ref[...].reshape(1, SUBLANES, cg)
    xc3 = x3 * cw_ref[CONV_WIDTH - 1:CONV_WIDTH, :] + cb_ref[...]
    for s in range(1, CONV_WIDTH):
        k = CONV_WIDTH - 1 - s
        rot = pltpu.roll(x3, s, 1)
        prev = jnp.concatenate([pltpu.roll(tail3, s, 1), rot[:n8 - 1]], axis=0)
        xc3 = xc3 + jnp.where(sub < s, prev, rot) * cw_ref[k:k + 1, :]
    tail_ref[...] = xr[tt - SUBLANES:, :]
    xc = xc3.reshape(tt, cg)

    xcb = xc.astype(BF16)
    r_parts, i_parts = [], []
    for h in range(heads_per_step):
        cols = slice(h * hd, (h + 1) * hd)
        r_parts.append(_bdot(xcb[:, cols], wr_ref[h].astype(BF16)))
        i_parts.append(_bdot(xcb[:, cols], wi_ref[h].astype(BF16)))
    r = jax.nn.sigmoid(jnp.concatenate(r_parts, axis=1) + br_ref[...])
    ig = jax.nn.sigmoid(jnp.concatenate(i_parts, axis=1) + bi_ref[...])

    lam = lam_ref[...]
    softplus_neg_lam = jnp.maximum(-lam, 0.0) + jnp.log1p(jnp.exp(-jnp.abs(lam)))
    log_a = (-LRU_C * r) * softplus_neg_lam
    a = jnp.exp(log_a)
    th = jnp.tanh(log_a)
    mult = jnp.sqrt(-2.0 * th / (1.0 - th))
    b = mult * (ig * xc)

    a = a.reshape(n8, SUBLANES, cg)
    b = b.reshape(n8, SUBLANES, cg)
    for d in (1, 2, 4):
        keep = sub >= d
        a_prev = jnp.where(keep, pltpu.roll(a, d, 1), 1.0)
        b_prev = jnp.where(keep, pltpu.roll(b, d, 1), 0.0)
        b = a * b_prev + b
        a = a * a_prev

    carry = hc_ref[...]
    for j in range(n8):
        hj = a[j] * carry + b[j]
        hs_ref[j * SUBLANES:(j + 1) * SUBLANES, :] = hj
        carry = jnp.broadcast_to(hj[SUBLANES - 1:SUBLANES, :], hj.shape)
    hc_ref[...] = carry

    o_ref[...] = (hs_ref[...] * y_ref[...].astype(F32)).astype(o_ref.dtype)


def _rglru(y_bf, x_bf, bsz, conv_w, conv_b, w_r, b_r, w_i, b_i, lam, tt, heads_per_step):
    t, w = x_bf.shape
    hd = w // N_HEADS
    seq = t // bsz
    tt = min(tt, seq)
    hps = heads_per_step
    cg = hps * hd
    assert seq % tt == 0 and N_HEADS % hps == 0 and tt % SUBLANES == 0
    n_g, n_s = N_HEADS // hps, seq // tt
    row = lambda g, b, s: b * n_s + s
    vec = lambda g, b, s: (0, g)
    return pl.pallas_call(
        functools.partial(_rglru_kernel, heads_per_step=hps, hd=hd),
        grid=(n_g, bsz, n_s),
        in_specs=[pl.BlockSpec((tt, cg), lambda g, b, s: (row(g, b, s), g)),
                  pl.BlockSpec((tt, cg), lambda g, b, s: (row(g, b, s), g)),
                  pl.BlockSpec((CONV_WIDTH, cg), vec),
                  pl.BlockSpec((1, cg), vec),
                  pl.BlockSpec((hps, hd, hd), lambda g, b, s: (g, 0, 0)),
                  pl.BlockSpec((1, cg), vec),
                  pl.BlockSpec((hps, hd, hd), lambda g, b, s: (g, 0, 0)),
                  pl.BlockSpec((1, cg), vec),
                  pl.BlockSpec((1, cg), vec)],
        out_specs=pl.BlockSpec((tt, cg), lambda g, b, s: (row(g, b, s), g)),
        out_shape=jax.ShapeDtypeStruct((t, w), BF16),
        scratch_shapes=[pltpu.VMEM((SUBLANES, cg), F32),
                        pltpu.VMEM((SUBLANES, cg), F32),
                        pltpu.VMEM((tt, cg), F32)],
        compiler_params=_cparams(3),
        name="rglru",
    )(y_bf, x_bf, conv_w, conv_b.reshape(1, w), w_r, b_r.reshape(1, w),
      w_i, b_i.reshape(1, w), lam.reshape(1, w))


def _out_ln_kernel(x_ref, w_ref, h_ref, g_ref, b_ref, o_ref, *, alpha):
    y = alpha * h_ref[...] + _bdot(x_ref[...], w_ref[...])
    o_ref[...] = _layer_norm(y, g_ref[...], b_ref[...])


def _out_ln(x_bf, w_bf, h, ln_g, ln_b, alpha, tm):
    t, k = x_bf.shape
    d = w_bf.shape[1]
    tm = min(tm, t)
    assert t % tm == 0
    return pl.pallas_call(
        functools.partial(_out_ln_kernel, alpha=alpha),
        grid=(t // tm,),
        in_specs=[pl.BlockSpec((tm, k), lambda i: (i, 0)),
                  pl.BlockSpec((k, d), lambda i: (0, 0), pipeline_mode=pl.Buffered(1)),
                  pl.BlockSpec((tm, d), lambda i: (i, 0)),
                  pl.BlockSpec((1, d), lambda i: (0, 0)),
                  pl.BlockSpec((1, d), lambda i: (0, 0))],
        out_specs=pl.BlockSpec((tm, d), lambda i: (i, 0)),
        out_shape=jax.ShapeDtypeStruct((t, d), F32),
        compiler_params=_cparams(1),
        name="out_ln",
    )(x_bf, w_bf, h, ln_g.reshape(1, d), ln_b.reshape(1, d))


def _router_kernel(h_ref, wt_ref, rb_ref, e_ref, g_ref, p_ref, cnt_ref, carry_ref):
    n_e, tm = wt_ref.shape[0], h_ref.shape[0]

    @pl.when(pl.program_id(0) == 0)
    def _():
        carry_ref[...] = jnp.zeros_like(carry_ref)

    x = h_ref[...]
    xh = x.astype(BF16)
    xl = (x - xh.astype(F32)).astype(BF16)
    w = wt_ref[...]
    wh = w.astype(BF16)
    wl = (w - wh.astype(F32)).astype(BF16)
    dg = lambda a, b: lax.dot_general(a, b, NT_DIMS, preferred_element_type=F32)
    logits = dg(wh, xh) + dg(wh, xl) + dg(wl, xh) + rb_ref[...]

    eidx = lax.broadcasted_iota(I32, (n_e, tm), 0)
    work = logits
    tops, sel = [], []
    for _ in range(TOP_K):
        m = jnp.max(work, axis=0, keepdims=True)
        e = jnp.min(jnp.where(work == m, eidx, n_e), axis=0, keepdims=True)
        hit = eidx == e
        tops.append(m)
        sel.append(hit)
        e_ref[len(sel) - 1:len(sel), :] = e
        work = jnp.where(hit, -jnp.inf, work)

    ps = [jnp.exp(m - tops[0]) for m in tops]
    denom = ps[0] + ps[1] + ps[2] + ps[3]
    for k in range(TOP_K):
        g_ref[k:k + 1, :] = ps[k] / denom

    onehot = jnp.where(sel[0] | sel[1] | sel[2] | sel[3], 1.0, 0.0)
    before = (lax.broadcasted_iota(I32, (tm, tm), 0)
              < lax.broadcasted_iota(I32, (tm, tm), 1)).astype(BF16)
    rank = _bdot(onehot.astype(BF16), before) + carry_ref[:, 0:1]
    for k in range(TOP_K):
        pk = jnp.sum(jnp.where(sel[k], rank, 0.0), axis=0, keepdims=True)
        p_ref[k:k + 1, :] = pk.astype(I32)
    carry_ref[...] = carry_ref[...] + jnp.sum(onehot, axis=1, keepdims=True)
    cnt_ref[...] = carry_ref[...]


def _router(h, router_w, router_b, tm):
    t, d = h.shape
    n_e = router_w.shape[1]
    tm = min(tm, t)
    assert t % tm == 0
    tok = lambda i: (0, i)
    return pl.pallas_call(
        _router_kernel,
        grid=(t // tm,),
        in_specs=[pl.BlockSpec((tm, d), lambda i: (i, 0)),
                  pl.BlockSpec((n_e, d), lambda i: (0, 0)),
                  pl.BlockSpec((n_e, 1), lambda i: (0, 0))],
        out_specs=[pl.BlockSpec((TOP_K, tm), tok),
                   pl.BlockSpec((TOP_K, tm), tok),
                   pl.BlockSpec((TOP_K, tm), tok),
                   pl.BlockSpec((n_e, LANES), lambda i: (0, 0))],
        out_shape=[jax.ShapeDtypeStruct((TOP_K, t), I32),
                   jax.ShapeDtypeStruct((TOP_K, t), F32),
                   jax.ShapeDtypeStruct((TOP_K, t), I32),
                   jax.ShapeDtypeStruct((n_e, LANES), F32)],
        scratch_shapes=[pltpu.VMEM((n_e, LANES), F32)],
        compiler_params=_cparams(1),
        name="router",
    )(h, router_w.T, router_b.reshape(n_e, 1))


def _push_kernel(dest_ref, pad_lo_ref, pad_hi_ref, h_ref, xs_ref, pk_ref, z_ref, sem, zsem,
                 *, tm, n_pad):
    i = pl.program_id(0)
    pk_ref[...] = _pack_rows(h_ref[...])

    def row_copy(t, k):
        return pltpu.make_async_copy(pk_ref.at[pl.ds(t, 1)],
                                     xs_ref.at[pl.ds(dest_ref[0, k, t], 1)], sem)

    def issue(t, c):
        for k in range(TOP_K):
            row_copy(t, k).start()
        return c

    lax.fori_loop(0, tm, issue, 0)

    for k in range(TOP_K):
        pltpu.make_async_copy(pk_ref, xs_ref.at[pl.ds(0, tm)], sem).wait()

    @pl.when(i == pl.num_programs(0) - 1)
    def _():
        z_ref[...] = jnp.zeros_like(z_ref)

        def zero_copy(r):
            return pltpu.make_async_copy(z_ref.at[pl.ds(0, 1)], xs_ref.at[pl.ds(r, 1)], zsem)

        for e in range(n_pad):
            lo, hi = pad_lo_ref[e], pad_hi_ref[e]
            lax.fori_loop(lo, hi, lambda r, c: (zero_copy(r).start(), c)[1], 0)
            lax.fori_loop(lo, hi, lambda r, c: (zero_copy(r).wait(), c)[1], 0)


def _push(h, dest3, pad_lo, pad_hi, n_rows):
    t, d = h.shape
    n_steps, _, tm = dest3.shape
    return pl.pallas_call(
        functools.partial(_push_kernel, tm=tm, n_pad=pad_lo.shape[0]),
        grid=(n_steps,),
        in_specs=[pl.BlockSpec((1, TOP_K, tm), lambda i: (i, 0, 0), memory_space=pltpu.SMEM),
                  pl.BlockSpec(memory_space=pltpu.SMEM),
                  pl.BlockSpec(memory_space=pltpu.SMEM),
                  pl.BlockSpec((tm, d), lambda i: (i, 0))],
        out_specs=pl.BlockSpec(memory_space=pl.ANY),
        out_shape=jax.ShapeDtypeStruct((n_rows, d // 2), U32),
        scratch_shapes=[pltpu.VMEM((tm, d // 2), U32),
                        pltpu.VMEM((SUBLANES, d // 2), U32),
                        pltpu.SemaphoreType.DMA, pltpu.SemaphoreType.DMA],
        compiler_params=_cparams(1),
        name="push",
    )(dest3, pad_lo, pad_hi, h)


def _expert_kernel(te_ref, ts_ref, nu_ref, x_ref, wgu_ref, bgu_ref, wd_ref, bd_ref, o_ref, *, ff):
    used = pl.program_id(0) < nu_ref[0]

    @pl.when(used)
    def _():
        x_hi, x_lo = _unpack_rows(x_ref[...])
        x = jnp.concatenate([x_hi.astype(BF16), x_lo.astype(BF16)], axis=1)
        half = x.shape[0] // 2
        gu = jnp.concatenate([_bdot(x[:half], wgu_ref[...]), _bdot(x[half:], wgu_ref[...])],
                             axis=0) + bgu_ref[...]
        hg = jnp.minimum(gu[:, :ff], SWIGLU_LIMIT)
        hu = jnp.clip(gu[:, ff:], -SWIGLU_LIMIT, SWIGLU_LIMIT)
        hdn = hg * jax.nn.sigmoid(SWIGLU_ALPHA * hg) * (hu + 1.0)
        o_ref[...] = _pack_rows(_bdot(hdn.astype(BF16), wd_ref[...]) + bd_ref[...])

    @pl.when(jnp.logical_not(used))
    def _():
        o_ref[...] = jnp.zeros_like(o_ref)


def _experts(xs, tile_e, tile_src, n_used, w_gu, b_gu, w_down, b_down, tr):
    n_rows, d2 = xs.shape
    n_e, d, ff2 = w_gu.shape
    ff = ff2 // 2
    n_tiles = n_rows // tr
    grid_spec = pltpu.PrefetchScalarGridSpec(
        num_scalar_prefetch=3,
        grid=(n_tiles,),
        in_specs=[pl.BlockSpec((tr, d2), lambda i, te, ts, nu: (ts[i], 0)),
                  pl.BlockSpec((None, d, ff2), lambda i, te, ts, nu: (te[i], 0, 0)),
                  pl.BlockSpec((None, 1, ff2), lambda i, te, ts, nu: (te[i], 0, 0)),
                  pl.BlockSpec((None, ff, d), lambda i, te, ts, nu: (te[i], 0, 0)),
                  pl.BlockSpec((None, 1, d), lambda i, te, ts, nu: (te[i], 0, 0))],
        out_specs=pl.BlockSpec((tr, d2), lambda i, te, ts, nu: (i, 0)),
    )
    return pl.pallas_call(
        functools.partial(_expert_kernel, ff=ff),
        grid_spec=grid_spec,
        out_shape=jax.ShapeDtypeStruct((n_rows, d2), U32),
        compiler_params=_cparams(1),
        name="experts",
    )(tile_e, tile_src, n_used, xs, w_gu, b_gu.reshape(n_e, 1, ff2), w_down,
      b_down.reshape(n_e, 1, d))


def _combine_kernel(dcur_ref, dnext_ref, ys_ref, h_ref, gate_ref, g_ref, b_ref,
                    o_ref, obf_ref, buf_ref, sem, *, alpha, tm):
    i = pl.program_id(0)
    n = pl.num_programs(0)
    slot = i % 2

    def row_copy(dref, s, t, k):
        return pltpu.make_async_copy(ys_ref.at[pl.ds(dref[0, k, t], 1)],
                                     buf_ref.at[s, k, pl.ds(t, 1)], sem.at[s])

    def issue(dref, s):
        def body(t, c):
            for k in range(TOP_K):
                row_copy(dref, s, t, k).start()
            return c
        lax.fori_loop(0, tm, body, 0)

    @pl.when(i == 0)
    def _():
        issue(dcur_ref, 0)

    @pl.when(i + 1 < n)
    def _():
        issue(dnext_ref, 1 - slot)

    for k in range(TOP_K):
        pltpu.make_async_copy(ys_ref.at[pl.ds(0, tm)], buf_ref.at[slot, k], sem.at[slot]).wait()

    gate = gate_ref[...]
    d2 = buf_ref.shape[-1]
    y_hi = alpha * h_ref[:, :d2]
    y_lo = alpha * h_ref[:, d2:]
    for k in range(TOP_K):
        e_hi, e_lo = _unpack_rows(buf_ref[slot, k])
        y_hi = y_hi + gate[:, k:k + 1] * e_hi
        y_lo = y_lo + gate[:, k:k + 1] * e_lo
    out = _layer_norm(jnp.concatenate([y_hi, y_lo], axis=1), g_ref[...], b_ref[...])
    o_ref[...] = out
    obf_ref[...] = out.astype(obf_ref.dtype)


def _combine(ys, dest3, h, gate_t, ln_g, ln_b, alpha):
    t, d = h.shape
    n_steps, _, tm = dest3.shape
    smem = lambda f: pl.BlockSpec((1, TOP_K, tm), f, memory_space=pltpu.SMEM)
    return pl.pallas_call(
        functools.partial(_combine_kernel, alpha=alpha, tm=tm),
        grid=(n_steps,),
        in_specs=[smem(lambda i: (i, 0, 0)),
                  smem(lambda i: (jnp.minimum(i + 1, n_steps - 1), 0, 0)),
                  pl.BlockSpec(memory_space=pl.ANY),
                  pl.BlockSpec((tm, d), lambda i: (i, 0)),
                  pl.BlockSpec((tm, TOP_K), lambda i: (i, 0)),
                  pl.BlockSpec((1, d), lambda i: (0, 0)),
                  pl.BlockSpec((1, d), lambda i: (0, 0))],
        out_specs=[pl.BlockSpec((tm, d), lambda i: (i, 0)),
                   pl.BlockSpec((tm, d), lambda i: (i, 0))],
        out_shape=[jax.ShapeDtypeStruct((t, d), F32),
                   jax.ShapeDtypeStruct((t, d), BF16)],
        scratch_shapes=[pltpu.VMEM((2, TOP_K, tm, d // 2), U32),
                        pltpu.SemaphoreType.DMA((2,))],
        compiler_params=_cparams(1),
        name="combine",
    )(dest3, dest3, ys, h, gate_t, ln_g.reshape(1, d), ln_b.reshape(1, d))


PROJ_TM, PROJ_TN = 1024, 1024
PROJ_CHUNKS = 4
SGU_ROWS = 512
RGLRU_TT, RGLRU_HEADS = 512, 2
OUT_TM = 128
ROUTER_TM = 512
PUSH_TM = 512
EXPERT_ROWS = 256
COMBINE_TM = 256


def _blocked_index(a, tm):
    k, t = a.shape
    return a.reshape(k, t // tm, tm).transpose(1, 0, 2)


def _moe(h, alpha, router_w, router_b, w_gu, b_gu, w_down, b_down, ln_g, ln_b):
    t, d = h.shape
    n_e = router_w.shape[1]
    tr = EXPERT_ROWS
    top_e, gate, pos, cnt = _router(h, router_w, router_b, ROUTER_TM)

    counts = cnt[:, 0].astype(I32)
    padded = (counts + tr - 1) // tr * tr
    pad_end = jnp.cumsum(padded)
    pad_start = pad_end - padded
    expert_id = jnp.arange(n_e, dtype=I32)[:, None, None]
    dest = pos + jnp.sum(jnp.where(top_e[None] == expert_id, pad_start[:, None, None], 0), axis=0)
    n_tiles = (t * TOP_K + n_e * (tr - 1)) // tr + 1
    n_used = (pad_end[-1] // tr).astype(I32)
    tile_id = jnp.arange(n_tiles, dtype=I32)
    tile_e = jnp.sum((pad_end[None, :] <= (tile_id * tr)[:, None]).astype(I32), axis=1)
    tile_e = jnp.minimum(tile_e, n_e - 1)
    tile_src = jnp.minimum(tile_id, n_used - 1)

    n_rows = n_tiles * tr
    pad_lo = jnp.concatenate([pad_start + counts, pad_end[-1:]]).astype(I32)
    pad_hi = jnp.concatenate([pad_end, jnp.full((1,), n_rows)]).astype(I32)
    xs = _push(h, _blocked_index(dest, min(PUSH_TM, t)), pad_lo, pad_hi, n_rows)
    ys = _experts(xs, tile_e, tile_src, n_used.reshape(1), w_gu, b_gu, w_down, b_down, tr)
    return _combine(ys, _blocked_index(dest, min(COMBINE_TM, t)), h, gate.T, ln_g, ln_b, alpha)


def kernel(x, a_w_in, a_ln_g, a_ln_b, a_w_s, a_b_s, a_w_out, b_w_in, b_conv_w, b_conv_b, b_w_r, b_b_r, b_w_i, b_b_i, b_lambda, b_w_out, ln1_g, ln1_b, ln2_g, ln2_b, router_w, router_b, ex_w_gu, ex_b_gu, ex_w_down, ex_b_down):
    bsz, seq, d = x.shape
    depth = ln1_g.shape[0]
    alpha = (2 * depth) ** 0.25
    h = x.reshape(bsz * seq, d)
    h_bf = h.astype(BF16)
    for layer in range(depth):
        j = layer // 2
        if layer % 2 == 0:
            z = _proj(h_bf, a_w_in, j, 0, 2 * d, True, PROJ_TM, PROJ_TN)
            mixed = _sgu(z, a_ln_g[j], a_ln_b[j], a_w_s[j], a_b_s[j], SGU_ROWS)
            w_out = a_w_out[j]
        else:
            y = _proj(h_bf, b_w_in, j, 0, d, True, PROJ_TM, PROJ_TN)
            xr = _proj(h_bf, b_w_in, j, d, d, False, PROJ_TM, PROJ_TN)
            mixed = _rglru(y, xr, bsz, b_conv_w[j], b_conv_b[j], b_w_r[j], b_b_r[j],
                           b_w_i[j], b_b_i[j], b_lambda[j], RGLRU_TT, RGLRU_HEADS)
            w_out = b_w_out[j]
        h1 = _out_ln(mixed, w_out.astype(BF16), h, ln1_g[layer], ln1_b[layer], alpha, OUT_TM)
        h, h_bf = _moe(h1, alpha, router_w[layer], router_b[layer],
                       ex_w_gu[layer].astype(BF16), ex_b_gu[layer],
                       ex_w_down[layer].astype(BF16), ex_b_down[layer],
                       ln2_g[layer], ln2_b[layer])
    return h.reshape(bsz, seq, d)
```

```python
import functools

import jax
import jax.numpy as jnp
from jax import lax
from jax.experimental import pallas as pl
from jax.experimental.pallas import tpu as pltpu

F32 = jnp.float32
BF16 = jnp.bfloat16
I32 = jnp.int32
U32 = jnp.uint32

CHUNK = 128
N_HEADS = 16
CONV_WIDTH = 4
LRU_C = 8.0
TOP_K = 4
SWIGLU_LIMIT = 7.0
SWIGLU_ALPHA = 1.702
LN_EPS = 1e-5

LANES = 128
SUBLANES = 8
VMEM_LIMIT_BYTES = 60000 * 1024

NT_DIMS = (((1,), (1,)), ((), ()))


def _cparams(n_axes, vmem=VMEM_LIMIT_BYTES):
    return pltpu.CompilerParams(
        dimension_semantics=("arbitrary",) * n_axes, vmem_limit_bytes=vmem)


def _gelu(x):
    return 0.5 * x * (1.0 + jnp.tanh(0.7978845608028654 * (x + 0.044715 * (x * x * x))))


def _layer_norm(x, g, b):
    mu = jnp.mean(x, axis=-1, keepdims=True)
    xc = x - mu
    var = jnp.mean(xc * xc, axis=-1, keepdims=True)
    return xc * lax.rsqrt(var + LN_EPS) * g + b


def _bdot(a, b):
    return jnp.dot(a, b, preferred_element_type=F32)


def _pack_rows(x):
    n2 = x.shape[1] // 2
    hi = lax.bitcast_convert_type(x[:, :n2].astype(BF16).astype(F32), U32)
    lo = lax.bitcast_convert_type(x[:, n2:].astype(BF16).astype(F32), U32)
    return hi | (lo >> 16)


def _unpack_rows(p):
    hi = lax.bitcast_convert_type(p & jnp.uint32(0xFFFF0000), F32)
    lo = lax.bitcast_convert_type(p << 16, F32)
    return hi, lo


def _proj_kernel(x_ref, w_ref, o_ref, wbf_ref, *, gelu, n_chunks):
    @pl.when(pl.program_id(1) == 0)
    def _():
        wbf_ref[...] = w_ref[...].astype(BF16)

    rows_per_chunk = x_ref.shape[0] // n_chunks
    for c in range(n_chunks):
        rows = slice(c * rows_per_chunk, (c + 1) * rows_per_chunk)
        acc = _bdot(x_ref[rows, :], wbf_ref[...])
        o_ref[rows, :] = (_gelu(acc) if gelu else acc).astype(o_ref.dtype)


def _proj(x_bf, w, layer, col0, n, gelu, tm, tn):
    t, k = x_bf.shape
    tm, tn = min(tm, t), min(tn, n)
    assert t % tm == 0 and n % tn == 0 and col0 % tn == 0
    return pl.pallas_call(
        functools.partial(_proj_kernel, gelu=gelu, n_chunks=PROJ_CHUNKS),
        grid=(n // tn, t // tm),
        in_specs=[pl.BlockSpec((tm, k), lambda j, i: (i, 0)),
                  pl.BlockSpec((None, k, tn), lambda j, i: (layer, 0, col0 // tn + j),
                               pipeline_mode=pl.Buffered(1))],
        out_specs=pl.BlockSpec((tm, tn), lambda j, i: (i, j)),
        out_shape=jax.ShapeDtypeStruct((t, n), BF16),
        scratch_shapes=[pltpu.VMEM((k, tn), BF16)],
        compiler_params=_cparams(2),
        name="proj",
    )(x_bf, w)


def _sgu_kernel(u_ref, v_ref, g_ref, b_ref, ws_ref, bs_ref, o_ref, *, n_chunks, hd):
    row = lax.broadcasted_iota(I32, (CHUNK, CHUNK), 0)
    col = lax.broadcasted_iota(I32, (CHUNK, CHUNK), 1)
    causal = row >= col
    for c in range(n_chunks):
        rows = slice(c * CHUNK, (c + 1) * CHUNK)
        vn = _layer_norm(v_ref[rows, :].astype(F32), g_ref[...], b_ref[...]).astype(BF16)
        for h in range(N_HEADS):
            cols = slice(h * hd, (h + 1) * hd)
            wm = jnp.where(causal, ws_ref[h], 0.0).astype(BF16)
            sv = _bdot(wm, vn[:, cols]) + bs_ref[:, h:h + 1]
            o_ref[rows, cols] = (u_ref[rows, cols].astype(F32) * sv).astype(o_ref.dtype)


def _sgu(z_bf, ln_g, ln_b, w_s, b_s, rows_per_step):
    t, w2 = z_bf.shape
    w = w2 // 2
    hd = w // N_HEADS
    r = min(rows_per_step, t)
    assert t % r == 0 and r % CHUNK == 0
    return pl.pallas_call(
        functools.partial(_sgu_kernel, n_chunks=r // CHUNK, hd=hd),
        grid=(t // r,),
        in_specs=[pl.BlockSpec((r, w), lambda i: (i, 0)),
                  pl.BlockSpec((r, w), lambda i: (i, 1)),
                  pl.BlockSpec((1, w), lambda i: (0, 0)),
                  pl.BlockSpec((1, w), lambda i: (0, 0)),
                  pl.BlockSpec((N_HEADS, CHUNK, CHUNK), lambda i: (0, 0, 0)),
                  pl.BlockSpec((CHUNK, N_HEADS), lambda i: (0, 0))],
        out_specs=pl.BlockSpec((r, w), lambda i: (i, 0)),
        out_shape=jax.ShapeDtypeStruct((t, w), BF16),
        compiler_params=_cparams(1),
        name="sgu",
    )(z_bf, z_bf, ln_g.reshape(1, w), ln_b.reshape(1, w), w_s, b_s.T)


def _rglru_kernel(y_ref, x_ref, cw_ref, cb_ref, wr_ref, br_ref, wi_ref, bi_ref, lam_ref,
                  o_ref, tail_ref, hc_ref, hs_ref, *, heads_per_step, hd):
    tt, cg = x_ref.shape
    n8 = tt // SUBLANES

    @pl.when(pl.program_id(2) == 0)
    def _():
        tail_ref[...] = jnp.zeros_like(tail_ref)
        hc_ref[...] = jnp.zeros_like(hc_ref)

    sub = lax.broadcasted_iota(I32, (1, SUBLANES, 1), 1)
    xr = x_ref[...].astype(F32)
    x3 = xr.reshape(n8, SUBLANES, cg)
    tail3 = tail_ref[...].reshape(1, SUBLANES, cg)
    xc3 = x3 * cw_ref[CONV_WIDTH - 1:CONV_WIDTH, :] + cb_ref[...]
    for s in range(1, CONV_WIDTH):
        k = CONV_WIDTH - 1 - s
        rot = pltpu.roll(x3, s, 1)
        prev = jnp.concatenate([pltpu.roll(tail3, s, 1), rot[:n8 - 1]], axis=0)
        xc3 = xc3 + jnp.where(sub < s, prev, rot) * cw_ref[k:k + 1, :]
    tail_ref[...] = xr[tt - SUBLANES:, :]
    xc = xc3.reshape(tt, cg)

    xcb = xc.astype(BF16)
    r_parts, i_parts = [], []
    for h in range(heads_per_step):
        cols = slice(h * hd, (h + 1) * hd)
        r_parts.append(_bdot(xcb[:, cols], wr_ref[h].astype(BF16)))
        i_parts.append(_bdot(xcb[:, cols], wi_ref[h].astype(BF16)))
    r = jax.nn.sigmoid(jnp.concatenate(r_parts, axis=1) + br_ref[...])
    ig = jax.nn.sigmoid(jnp.concatenate(i_parts, axis=1) + bi_ref[...])

    lam = lam_ref[...]
    softplus_neg_lam = jnp.maximum(-lam, 0.0) + jnp.log1p(jnp.exp(-jnp.abs(lam)))
    log_a = (-LRU_C * r) * softplus_neg_lam
    a = jnp.exp(log_a)
    th = jnp.tanh(log_a)
    mult = jnp.sqrt(-2.0 * th / (1.0 - th))
    b = mult * (ig * xc)

    a = a.reshape(n8, SUBLANES, cg)
    b = b.reshape(n8, SUBLANES, cg)
    for d in (1, 2, 4):
        keep = sub >= d
        a_prev = jnp.where(keep, pltpu.roll(a, d, 1), 1.0)
        b_prev = jnp.where(keep, pltpu.roll(b, d, 1), 0.0)
        b = a * b_prev + b
        a = a * a_prev

    carry = hc_ref[...]
    for j in range(n8):
        hj = a[j] * carry + b[j]
        hs_ref[j * SUBLANES:(j + 1) * SUBLANES, :] = hj
        carry = jnp.broadcast_to(hj[SUBLANES - 1:SUBLANES, :], hj.shape)
    hc_ref[...] = carry

    o_ref[...] = (hs_ref[...] * y_ref[...].astype(F32)).astype(o_ref.dtype)


def _rglru(y_bf, x_bf, bsz, conv_w, conv_b, w_r, b_r, w_i, b_i, lam, tt, heads_per_step):
    t, w = x_bf.shape
    hd = w // N_HEADS
    seq = t // bsz
    tt = min(tt, seq)
    hps = heads_per_step
    cg = hps * hd
    assert seq % tt == 0 and N_HEADS % hps == 0 and tt % SUBLANES == 0
    n_g, n_s = N_HEADS // hps, seq // tt
    row = lambda g, b, s: b * n_s + s
    vec = lambda g, b, s: (0, g)
    return pl.pallas_call(
        functools.partial(_rglru_kernel, heads_per_step=hps, hd=hd),
        grid=(n_g, bsz, n_s),
        in_specs=[pl.BlockSpec((tt, cg), lambda g, b, s: (row(g, b, s), g)),
                  pl.BlockSpec((tt, cg), lambda g, b, s: (row(g, b, s), g)),
                  pl.BlockSpec((CONV_WIDTH, cg), vec),
                  pl.BlockSpec((1, cg), vec),
                  pl.BlockSpec((hps, hd, hd), lambda g, b, s: (g, 0, 0)),
                  pl.BlockSpec((1, cg), vec),
                  pl.BlockSpec((hps, hd, hd), lambda g, b, s: (g, 0, 0)),
                  pl.BlockSpec((1, cg), vec),
                  pl.BlockSpec((1, cg), vec)],
        out_specs=pl.BlockSpec((tt, cg), lambda g, b, s: (row(g, b, s), g)),
        out_shape=jax.ShapeDtypeStruct((t, w), BF16),
        scratch_shapes=[pltpu.VMEM((SUBLANES, cg), F32),
                        pltpu.VMEM((SUBLANES, cg), F32),
                        pltpu.VMEM((tt, cg), F32)],
        compiler_params=_cparams(3),
        name="rglru",
    )(y_bf, x_bf, conv_w, conv_b.reshape(1, w), w_r, b_r.reshape(1, w),
      w_i, b_i.reshape(1, w), lam.reshape(1, w))


def _out_ln_kernel(x_ref, w_ref, h_ref, g_ref, b_ref, o_ref, *, alpha):
    y = alpha * h_ref[...] + _bdot(x_ref[...], w_ref[...])
    o_ref[...] = _layer_norm(y, g_ref[...], b_ref[...])


def _out_ln(x_bf, w_bf, h, ln_g, ln_b, alpha, tm):
    t, k = x_bf.shape
    d = w_bf.shape[1]
    tm = min(tm, t)
    assert t % tm == 0
    return pl.pallas_call(
        functools.partial(_out_ln_kernel, alpha=alpha),
        grid=(t // tm,),
        in_specs=[pl.BlockSpec((tm, k), lambda i: (i, 0)),
                  pl.BlockSpec((k, d), lambda i: (0, 0), pipeline_mode=pl.Buffered(1)),
                  pl.BlockSpec((tm, d), lambda i: (i, 0)),
                  pl.BlockSpec((1, d), lambda i: (0, 0)),
                  pl.BlockSpec((1, d), lambda i: (0, 0))],
        out_specs=pl.BlockSpec((tm, d), lambda i: (i, 0)),
        out_shape=jax.ShapeDtypeStruct((t, d), F32),
        compiler_params=_cparams(1),
        name="out_ln",
    )(x_bf, w_bf, h, ln_g.reshape(1, d), ln_b.reshape(1, d))


def _router_kernel(h_ref, wt_ref, rb_ref, e_ref, g_ref, p_ref, cnt_ref, carry_ref):
    n_e, tm = wt_ref.shape[0], h_ref.shape[0]

    @pl.when(pl.program_id(0) == 0)
    def _():
        carry_ref[...] = jnp.zeros_like(carry_ref)

    x = h_ref[...]
    xh = x.astype(BF16)
    xl = (x - xh.astype(F32)).astype(BF16)
    w = wt_ref[...]
    wh = w.astype(BF16)
    wl = (w - wh.astype(F32)).astype(BF16)
    dg = lambda a, b: lax.dot_general(a, b, NT_DIMS, preferred_element_type=F32)
    logits = dg(wh, xh) + dg(wh, xl) + dg(wl, xh) + rb_ref[...]

    eidx = lax.broadcasted_iota(I32, (n_e, tm), 0)
    work = logits
    tops, sel = [], []
    for _ in range(TOP_K):
        m = jnp.max(work, axis=0, keepdims=True)
        e = jnp.min(jnp.where(work == m, eidx, n_e), axis=0, keepdims=True)
        hit = eidx == e
        tops.append(m)
        sel.append(hit)
        e_ref[len(sel) - 1:len(sel), :] = e
        work = jnp.where(hit, -jnp.inf, work)

    ps = [jnp.exp(m - tops[0]) for m in tops]
    denom = ps[0] + ps[1] + ps[2] + ps[3]
    for k in range(TOP_K):
        g_ref[k:k + 1, :] = ps[k] / denom

    onehot = jnp.where(sel[0] | sel[1] | sel[2] | sel[3], 1.0, 0.0)
    before = (lax.broadcasted_iota(I32, (tm, tm), 0)
              < lax.broadcasted_iota(I32, (tm, tm), 1)).astype(BF16)
    rank = _bdot(onehot.astype(BF16), before) + carry_ref[:, 0:1]
    for k in range(TOP_K):
        pk = jnp.sum(jnp.where(sel[k], rank, 0.0), axis=0, keepdims=True)
        p_ref[k:k + 1, :] = pk.astype(I32)
    carry_ref[...] = carry_ref[...] + jnp.sum(onehot, axis=1, keepdims=True)
    cnt_ref[...] = carry_ref[...]


def _router(h, router_w, router_b, tm):
    t, d = h.shape
    n_e = router_w.shape[1]
    tm = min(tm, t)
    assert t % tm == 0
    tok = lambda i: (0, i)
    return pl.pallas_call(
        _router_kernel,
        grid=(t // tm,),
        in_specs=[pl.BlockSpec((tm, d), lambda i: (i, 0)),
                  pl.BlockSpec((n_e, d), lambda i: (0, 0)),
                  pl.BlockSpec((n_e, 1), lambda i: (0, 0))],
        out_specs=[pl.BlockSpec((TOP_K, tm), tok),
                   pl.BlockSpec((TOP_K, tm), tok),
                   pl.BlockSpec((TOP_K, tm), tok),
                   pl.BlockSpec((n_e, LANES), lambda i: (0, 0))],
        out_shape=[jax.ShapeDtypeStruct((TOP_K, t), I32),
                   jax.ShapeDtypeStruct((TOP_K, t), F32),
                   jax.ShapeDtypeStruct((TOP_K, t), I32),
                   jax.ShapeDtypeStruct((n_e, LANES), F32)],
        scratch_shapes=[pltpu.VMEM((n_e, LANES), F32)],
        compiler_params=_cparams(1),
        name="router",
    )(h, router_w.T, router_b.reshape(n_e, 1))


def _push_kernel(dest_ref, pad_lo_ref, pad_hi_ref, h_ref, xs_ref, pk_ref, z_ref, sem, zsem,
                 *, tm, n_pad):
    i = pl.program_id(0)
    pk_ref[...] = _pack_rows(h_ref[...])

    def row_copy(t, k):
        return pltpu.make_async_copy(pk_ref.at[pl.ds(t, 1)],
                                     xs_ref.at[pl.ds(dest_ref[0, k, t], 1)], sem)

    def issue(j, c):
        t0 = pl.multiple_of(j * PUSH_GROUP, PUSH_GROUP)
        for u in range(PUSH_GROUP):
            for k in range(TOP_K):
                row_copy(t0 + u, k).start()
        return c

    lax.fori_loop(0, tm // PUSH_GROUP, issue, 0)

    for k in range(TOP_K):
        pltpu.make_async_copy(pk_ref, xs_ref.at[pl.ds(0, tm)], sem).wait()

    @pl.when(i == pl.num_programs(0) - 1)
    def _():
        z_ref[...] = jnp.zeros_like(z_ref)

        def zero_copy(r):
            return pltpu.make_async_copy(z_ref.at[pl.ds(0, 1)], xs_ref.at[pl.ds(r, 1)], zsem)

        for e in range(n_pad):
            lo, hi = pad_lo_ref[e], pad_hi_ref[e]
            lax.fori_loop(lo, hi, lambda r, c: (zero_copy(r).start(), c)[1], 0)
            lax.fori_loop(lo, hi, lambda r, c: (zero_copy(r).wait(), c)[1], 0)


def _push(h, dest3, pad_lo, pad_hi, n_rows):
    t, d = h.shape
    n_steps, _, tm = dest3.shape
    return pl.pallas_call(
        functools.partial(_push_kernel, tm=tm, n_pad=pad_lo.shape[0]),
        grid=(n_steps,),
        in_specs=[pl.BlockSpec((1, TOP_K, tm), lambda i: (i, 0, 0), memory_space=pltpu.SMEM),
                  pl.BlockSpec(memory_space=pltpu.SMEM),
                  pl.BlockSpec(memory_space=pltpu.SMEM),
                  pl.BlockSpec((tm, d), lambda i: (i, 0))],
        out_specs=pl.BlockSpec(memory_space=pl.ANY),
        out_shape=jax.ShapeDtypeStruct((n_rows, d // 2), U32),
        scratch_shapes=[pltpu.VMEM((tm, d // 2), U32),
                        pltpu.VMEM((SUBLANES, d // 2), U32),
                        pltpu.SemaphoreType.DMA, pltpu.SemaphoreType.DMA],
        compiler_params=_cparams(1),
        name="push",
    )(dest3, pad_lo, pad_hi, h)


def _expert_kernel(te_ref, ts_ref, nu_ref, x_ref, wgu_ref, bgu_ref, wd_ref, bd_ref, o_ref, *, ff):
    used = pl.program_id(0) < nu_ref[0]

    @pl.when(used)
    def _():
        x_hi, x_lo = _unpack_rows(x_ref[...])
        x = jnp.concatenate([x_hi.astype(BF16), x_lo.astype(BF16)], axis=1)
        half = x.shape[0] // 2
        gu = jnp.concatenate([_bdot(x[:half], wgu_ref[...]), _bdot(x[half:], wgu_ref[...])],
                             axis=0) + bgu_ref[...]
        hg = jnp.minimum(gu[:, :ff], SWIGLU_LIMIT)
        hu = jnp.clip(gu[:, ff:], -SWIGLU_LIMIT, SWIGLU_LIMIT)
        hdn = hg * jax.nn.sigmoid(SWIGLU_ALPHA * hg) * (hu + 1.0)
        o_ref[...] = _pack_rows(_bdot(hdn.astype(BF16), wd_ref[...]) + bd_ref[...])

    @pl.when(jnp.logical_not(used))
    def _():
        o_ref[...] = jnp.zeros_like(o_ref)


def _experts(xs, tile_e, tile_src, n_used, layer, w_gu, b_gu, w_down, b_down, tr):
    n_rows, d2 = xs.shape
    _, n_e, d, ff2 = w_gu.shape
    ff = ff2 // 2
    n_tiles = n_rows // tr
    expert = lambda i, te, ts, nu: (layer, te[i], 0, 0)
    grid_spec = pltpu.PrefetchScalarGridSpec(
        num_scalar_prefetch=3,
        grid=(n_tiles,),
        in_specs=[pl.BlockSpec((tr, d2), lambda i, te, ts, nu: (ts[i], 0)),
                  pl.BlockSpec((None, None, d, ff2), expert),
                  pl.BlockSpec((None, 1, ff2), lambda i, te, ts, nu: (te[i], 0, 0)),
                  pl.BlockSpec((None, None, ff, d), expert),
                  pl.BlockSpec((None, 1, d), lambda i, te, ts, nu: (te[i], 0, 0))],
        out_specs=pl.BlockSpec((tr, d2), lambda i, te, ts, nu: (i, 0)),
    )
    return pl.pallas_call(
        functools.partial(_expert_kernel, ff=ff),
        grid_spec=grid_spec,
        out_shape=jax.ShapeDtypeStruct((n_rows, d2), U32),
        compiler_params=_cparams(1),
        name="experts",
    )(tile_e, tile_src, n_used, xs, w_gu, b_gu.reshape(n_e, 1, ff2), w_down,
      b_down.reshape(n_e, 1, d))


def _combine_kernel(dcur_ref, dnext_ref, ys_ref, h_ref, gate_ref, g_ref, b_ref,
                    o_ref, obf_ref, buf_ref, sem, *, alpha, tm):
    i = pl.program_id(0)
    n = pl.num_programs(0)
    slot = i % 2
    d2 = buf_ref.shape[-1]
    grp = COMBINE_GROUP

    def start_group(dref, s, r0):
        for u in range(grp):
            for k in range(TOP_K):
                pltpu.make_async_copy(ys_ref.at[pl.ds(dref[0, k, r0 + u], 1)],
                                      buf_ref.at[s, k, pl.ds(r0 + u, 1)], sem.at[s]).start()

    def wait_slot(s):
        for k in range(TOP_K):
            pltpu.make_async_copy(ys_ref.at[pl.ds(0, tm)], buf_ref.at[s, k], sem.at[s]).wait()

    @pl.when(i == 0)
    def _():
        lax.fori_loop(0, tm // grp,
                      lambda j, c: (start_group(dcur_ref, 0, pl.multiple_of(j * grp, grp)), c)[1], 0)

    wait_slot(slot)

    def body(j, c):
        r0 = pl.multiple_of(j * grp, grp)
        rows = pl.ds(r0, grp)
        gate = gate_ref[rows, :]
        y_hi = alpha * h_ref[rows, :d2]
        y_lo = alpha * h_ref[rows, d2:]
        for k in range(TOP_K):
            e_hi, e_lo = _unpack_rows(buf_ref[slot, k, rows, :])
            y_hi = y_hi + gate[:, k:k + 1] * e_hi
            y_lo = y_lo + gate[:, k:k + 1] * e_lo
        out = _layer_norm(jnp.concatenate([y_hi, y_lo], axis=1), g_ref[...], b_ref[...])
        o_ref[rows, :] = out
        obf_ref[rows, :] = out.astype(obf_ref.dtype)
        start_group(dnext_ref, 1 - slot, r0)
        return c

    lax.fori_loop(0, tm // grp, body, 0)

    @pl.when(i == n - 1)
    def _():
        wait_slot(1 - slot)


def _combine(ys, dest3, h, gate_t, ln_g, ln_b, alpha):
    t, d = h.shape
    n_steps, _, tm = dest3.shape
    smem = lambda f: pl.BlockSpec((1, TOP_K, tm), f, memory_space=pltpu.SMEM)
    return pl.pallas_call(
        functools.partial(_combine_kernel, alpha=alpha, tm=tm),
        grid=(n_steps,),
        in_specs=[smem(lambda i: (i, 0, 0)),
                  smem(lambda i: (jnp.minimum(i + 1, n_steps - 1), 0, 0)),
                  pl.BlockSpec(memory_space=pl.ANY),
                  pl.BlockSpec((tm, d), lambda i: (i, 0)),
                  pl.BlockSpec((tm, TOP_K), lambda i: (i, 0)),
                  pl.BlockSpec((1, d), lambda i: (0, 0)),
                  pl.BlockSpec((1, d), lambda i: (0, 0))],
        out_specs=[pl.BlockSpec((tm, d), lambda i: (i, 0)),
                   pl.BlockSpec((tm, d), lambda i: (i, 0))],
        out_shape=[jax.ShapeDtypeStruct((t, d), F32),
                   jax.ShapeDtypeStruct((t, d), BF16)],
        scratch_shapes=[pltpu.VMEM((2, TOP_K, tm, d // 2), U32),
                        pltpu.SemaphoreType.DMA((2,))],
        compiler_params=_cparams(1),
        name="combine",
    )(dest3, dest3, ys, h, gate_t, ln_g.reshape(1, d), ln_b.reshape(1, d))


PROJ_TM, PROJ_TN = 1024, 1024
PROJ_CHUNKS = 4
SGU_ROWS = 512
RGLRU_TT, RGLRU_HEADS = 512, 2
OUT_TM = 128
ROUTER_TM = 512
PUSH_TM = 512
EXPERT_ROWS = 256
COMBINE_TM = 256
COMBINE_GROUP = 16
PUSH_GROUP = 8


def _blocked_index(a, tm):
    k, t = a.shape
    return a.reshape(k, t // tm, tm).transpose(1, 0, 2)


def _moe(h, alpha, layer, router_w, router_b, w_gu, b_gu, w_down, b_down, ln_g, ln_b):
    t, d = h.shape
    n_e = router_w.shape[1]
    tr = EXPERT_ROWS
    top_e, gate, pos, cnt = _router(h, router_w, router_b, ROUTER_TM)

    counts = cnt[:, 0].astype(I32)
    padded = (counts + tr - 1) // tr * tr
    pad_end = jnp.cumsum(padded)
    pad_start = pad_end - padded
    expert_id = jnp.arange(n_e, dtype=I32)[:, None, None]
    dest = pos + jnp.sum(jnp.where(top_e[None] == expert_id, pad_start[:, None, None], 0), axis=0)
    n_tiles = (t * TOP_K + n_e * (tr - 1)) // tr + 1
    n_used = (pad_end[-1] // tr).astype(I32)
    tile_id = jnp.arange(n_tiles, dtype=I32)
    tile_e = jnp.sum((pad_end[None, :] <= (tile_id * tr)[:, None]).astype(I32), axis=1)
    tile_e = jnp.minimum(tile_e, n_e - 1)
    tile_src = jnp.minimum(tile_id, n_used - 1)

    n_rows = n_tiles * tr
    pad_lo = jnp.concatenate([pad_start + counts, pad_end[-1:]]).astype(I32)
    pad_hi = jnp.concatenate([pad_end, jnp.full((1,), n_rows)]).astype(I32)
    xs = _push(h, _blocked_index(dest, min(PUSH_TM, t)), pad_lo, pad_hi, n_rows)
    ys = _experts(xs, tile_e, tile_src, n_used.reshape(1), layer, w_gu, b_gu, w_down, b_down, tr)
    return _combine(ys, _blocked_index(dest, min(COMBINE_TM, t)), h, gate.T, ln_g, ln_b, alpha)


def kernel(x, a_w_in, a_ln_g, a_ln_b, a_w_s, a_b_s, a_w_out, b_w_in, b_conv_w, b_conv_b, b_w_r, b_b_r, b_w_i, b_b_i, b_lambda, b_w_out, ln1_g, ln1_b, ln2_g, ln2_b, router_w, router_b, ex_w_gu, ex_b_gu, ex_w_down, ex_b_down):
    bsz, seq, d = x.shape
    depth = ln1_g.shape[0]
    alpha = (2 * depth) ** 0.25
    h = x.reshape(bsz * seq, d)
    h_bf = h.astype(BF16)
    w_gu_bf = ex_w_gu.astype(BF16)
    w_down_bf = ex_w_down.astype(BF16)
    for layer in range(depth):
        j = layer // 2
        if layer % 2 == 0:
            z = _proj(h_bf, a_w_in, j, 0, 2 * d, True, PROJ_TM, PROJ_TN)
            mixed = _sgu(z, a_ln_g[j], a_ln_b[j], a_w_s[j], a_b_s[j], SGU_ROWS)
            w_out = a_w_out[j]
        else:
            y = _proj(h_bf, b_w_in, j, 0, d, True, PROJ_TM, PROJ_TN)
            xr = _proj(h_bf, b_w_in, j, d, d, False, PROJ_TM, PROJ_TN)
            mixed = _rglru(y, xr, bsz, b_conv_w[j], b_conv_b[j], b_w_r[j], b_b_r[j],
                           b_w_i[j], b_b_i[j], b_lambda[j], RGLRU_TT, RGLRU_HEADS)
            w_out = b_w_out[j]
        h1 = _out_ln(mixed, w_out.astype(BF16), h, ln1_g[layer], ln1_b[layer], alpha, OUT_TM)
        h, h_bf = _moe(h1, alpha, layer, router_w[layer], router_b[layer],
                       w_gu_bf, ex_b_gu[layer], w_down_bf, ex_b_down[layer],
                       ln2_g[layer], ln2_b[layer])
    return h.reshape(bsz, seq, d)
```

```python
import functools

import jax
import jax.numpy as jnp
from jax import lax
from jax.experimental import pallas as pl
from jax.experimental.pallas import tpu as pltpu

F32 = jnp.float32
BF16 = jnp.bfloat16
I32 = jnp.int32
U32 = jnp.uint32

CHUNK = 128
N_HEADS = 16
CONV_WIDTH = 4
LRU_C = 8.0
TOP_K = 4
SWIGLU_LIMIT = 7.0
SWIGLU_ALPHA = 1.702
LN_EPS = 1e-5

LANES = 128
SUBLANES = 8
VMEM_LIMIT_BYTES = 60000 * 1024

NT_DIMS = (((1,), (1,)), ((), ()))


def _cparams(n_axes, vmem=VMEM_LIMIT_BYTES):
    return pltpu.CompilerParams(
        dimension_semantics=("arbitrary",) * n_axes, vmem_limit_bytes=vmem)


def _gelu(x):
    return 0.5 * x * (1.0 + jnp.tanh(0.7978845608028654 * (x + 0.044715 * (x * x * x))))


def _layer_norm(x, g, b):
    mu = jnp.mean(x, axis=-1, keepdims=True)
    xc = x - mu
    var = jnp.mean(xc * xc, axis=-1, keepdims=True)
    return xc * lax.rsqrt(var + LN_EPS) * g + b


def _bdot(a, b):
    return jnp.dot(a, b, preferred_element_type=F32)


def _pack_rows(x):
    n2 = x.shape[1] // 2
    hi = lax.bitcast_convert_type(x[:, :n2].astype(BF16).astype(F32), U32)
    lo = lax.bitcast_convert_type(x[:, n2:].astype(BF16).astype(F32), U32)
    return hi | (lo >> 16)


def _unpack_rows(p):
    hi = lax.bitcast_convert_type(p & jnp.uint32(0xFFFF0000), F32)
    lo = lax.bitcast_convert_type(p << 16, F32)
    return hi, lo


def _proj_kernel(x_ref, w_ref, o_ref, wbf_ref, *, gelu, n_chunks):
    @pl.when(pl.program_id(1) == 0)
    def _():
        wbf_ref[...] = w_ref[...].astype(BF16)

    rows_per_chunk = x_ref.shape[0] // n_chunks
    for c in range(n_chunks):
        rows = slice(c * rows_per_chunk, (c + 1) * rows_per_chunk)
        acc = _bdot(x_ref[rows, :], wbf_ref[...])
        o_ref[rows, :] = (_gelu(acc) if gelu else acc).astype(o_ref.dtype)


def _proj(x_bf, w, layer, col0, n, gelu, tm, tn):
    t, k = x_bf.shape
    tm, tn = min(tm, t), min(tn, n)
    assert t % tm == 0 and n % tn == 0 and col0 % tn == 0
    return pl.pallas_call(
        functools.partial(_proj_kernel, gelu=gelu, n_chunks=PROJ_CHUNKS),
        grid=(n // tn, t // tm),
        in_specs=[pl.BlockSpec((tm, k), lambda j, i: (i, 0)),
                  pl.BlockSpec((None, k, tn), lambda j, i: (layer, 0, col0 // tn + j),
                               pipeline_mode=pl.Buffered(1))],
        out_specs=pl.BlockSpec((tm, tn), lambda j, i: (i, j)),
        out_shape=jax.ShapeDtypeStruct((t, n), BF16),
        scratch_shapes=[pltpu.VMEM((k, tn), BF16)],
        compiler_params=_cparams(2),
        name="proj",
    )(x_bf, w)


def _sgu_kernel(u_ref, v_ref, g_ref, b_ref, ws_ref, bs_ref, o_ref, *, n_chunks, hd):
    row = lax.broadcasted_iota(I32, (CHUNK, CHUNK), 0)
    col = lax.broadcasted_iota(I32, (CHUNK, CHUNK), 1)
    causal = row >= col
    for c in range(n_chunks):
        rows = slice(c * CHUNK, (c + 1) * CHUNK)
        vn = _layer_norm(v_ref[rows, :].astype(F32), g_ref[...], b_ref[...]).astype(BF16)
        for h in range(N_HEADS):
            cols = slice(h * hd, (h + 1) * hd)
            wm = jnp.where(causal, ws_ref[h], 0.0).astype(BF16)
            sv = _bdot(wm, vn[:, cols]) + bs_ref[:, h:h + 1]
            o_ref[rows, cols] = (u_ref[rows, cols].astype(F32) * sv).astype(o_ref.dtype)


def _sgu(z_bf, ln_g, ln_b, w_s, b_s, rows_per_step):
    t, w2 = z_bf.shape
    w = w2 // 2
    hd = w // N_HEADS
    r = min(rows_per_step, t)
    assert t % r == 0 and r % CHUNK == 0
    return pl.pallas_call(
        functools.partial(_sgu_kernel, n_chunks=r // CHUNK, hd=hd),
        grid=(t // r,),
        in_specs=[pl.BlockSpec((r, w), lambda i: (i, 0)),
                  pl.BlockSpec((r, w), lambda i: (i, 1)),
                  pl.BlockSpec((1, w), lambda i: (0, 0)),
                  pl.BlockSpec((1, w), lambda i: (0, 0)),
                  pl.BlockSpec((N_HEADS, CHUNK, CHUNK), lambda i: (0, 0, 0)),
                  pl.BlockSpec((CHUNK, N_HEADS), lambda i: (0, 0))],
        out_specs=pl.BlockSpec((r, w), lambda i: (i, 0)),
        out_shape=jax.ShapeDtypeStruct((t, w), BF16),
        compiler_params=_cparams(1),
        name="sgu",
    )(z_bf, z_bf, ln_g.reshape(1, w), ln_b.reshape(1, w), w_s, b_s.T)


def _rglru_kernel(y_ref, x_ref, cw_ref, cb_ref, wr_ref, br_ref, wi_ref, bi_ref, lam_ref,
                  o_ref, tail_ref, hc_ref, hs_ref, *, heads_per_step, hd):
    tt, cg = x_ref.shape
    n8 = tt // SUBLANES

    @pl.when(pl.program_id(2) == 0)
    def _():
        tail_ref[...] = jnp.zeros_like(tail_ref)
        hc_ref[...] = jnp.zeros_like(hc_ref)

    sub = lax.broadcasted_iota(I32, (1, SUBLANES, 1), 1)
    xr = x_ref[...].astype(F32)
    x3 = xr.reshape(n8, SUBLANES, cg)
    tail3 = tail_ref[...].reshape(1, SUBLANES, cg)
    xc3 = x3 * cw_ref[CONV_WIDTH - 1:CONV_WIDTH, :] + cb_ref[...]
    for s in range(1, CONV_WIDTH):
        k = CONV_WIDTH - 1 - s
        rot = pltpu.roll(x3, s, 1)
        prev = jnp.concatenate([pltpu.roll(tail3, s, 1), rot[:n8 - 1]], axis=0)
        xc3 = xc3 + jnp.where(sub < s, prev, rot) * cw_ref[k:k + 1, :]
    tail_ref[...] = xr[tt - SUBLANES:, :]
    xc = xc3.reshape(tt, cg)

    xcb = xc.astype(BF16)
    r_parts, i_parts = [], []
    for h in range(heads_per_step):
        cols = slice(h * hd, (h + 1) * hd)
        r_parts.append(_bdot(xcb[:, cols], wr_ref[h].astype(BF16)))
        i_parts.append(_bdot(xcb[:, cols], wi_ref[h].astype(BF16)))
    r = jax.nn.sigmoid(jnp.concatenate(r_parts, axis=1) + br_ref[...])
    ig = jax.nn.sigmoid(jnp.concatenate(i_parts, axis=1) + bi_ref[...])

    lam = lam_ref[...]
    softplus_neg_lam = jnp.maximum(-lam, 0.0) + jnp.log1p(jnp.exp(-jnp.abs(lam)))
    log_a = (-LRU_C * r) * softplus_neg_lam
    a = jnp.exp(log_a)
    th = jnp.tanh(log_a)
    mult = jnp.sqrt(-2.0 * th / (1.0 - th))
    b = mult * (ig * xc)

    a = a.reshape(n8, SUBLANES, cg)
    b = b.reshape(n8, SUBLANES, cg)
    for d in (1, 2, 4):
        keep = sub >= d
        a_prev = jnp.where(keep, pltpu.roll(a, d, 1), 1.0)
        b_prev = jnp.where(keep, pltpu.roll(b, d, 1), 0.0)
        b = a * b_prev + b
        a = a * a_prev

    carry = hc_ref[...]
    for j in range(n8):
        hj = a[j] * carry + b[j]
        hs_ref[j * SUBLANES:(j + 1) * SUBLANES, :] = hj
        carry = jnp.broadcast_to(hj[SUBLANES - 1:SUBLANES, :], hj.shape)
    hc_ref[...] = carry

    o_ref[...] = (hs_ref[...] * y_ref[...].astype(F32)).astype(o_ref.dtype)


def _rglru(y_bf, x_bf, bsz, conv_w, conv_b, w_r, b_r, w_i, b_i, lam, tt, heads_per_step):
    t, w = x_bf.shape
    hd = w // N_HEADS
    seq = t // bsz
    tt = min(tt, seq)
    hps = heads_per_step
    cg = hps * hd
    assert seq % tt == 0 and N_HEADS % hps == 0 and tt % SUBLANES == 0
    n_g, n_s = N_HEADS // hps, seq // tt
    row = lambda g, b, s: b * n_s + s
    vec = lambda g, b, s: (0, g)
    return pl.pallas_call(
        functools.partial(_rglru_kernel, heads_per_step=hps, hd=hd),
        grid=(n_g, bsz, n_s),
        in_specs=[pl.BlockSpec((tt, cg), lambda g, b, s: (row(g, b, s), g)),
                  pl.BlockSpec((tt, cg), lambda g, b, s: (row(g, b, s), g)),
                  pl.BlockSpec((CONV_WIDTH, cg), vec),
                  pl.BlockSpec((1, cg), vec),
                  pl.BlockSpec((hps, hd, hd), lambda g, b, s: (g, 0, 0)),
                  pl.BlockSpec((1, cg), vec),
                  pl.BlockSpec((hps, hd, hd), lambda g, b, s: (g, 0, 0)),
                  pl.BlockSpec((1, cg), vec),
                  pl.BlockSpec((1, cg), vec)],
        out_specs=pl.BlockSpec((tt, cg), lambda g, b, s: (row(g, b, s), g)),
        out_shape=jax.ShapeDtypeStruct((t, w), BF16),
        scratch_shapes=[pltpu.VMEM((SUBLANES, cg), F32),
                        pltpu.VMEM((SUBLANES, cg), F32),
                        pltpu.VMEM((tt, cg), F32)],
        compiler_params=_cparams(3),
        name="rglru",
    )(y_bf, x_bf, conv_w, conv_b.reshape(1, w), w_r, b_r.reshape(1, w),
      w_i, b_i.reshape(1, w), lam.reshape(1, w))


def _out_ln_kernel(x_ref, w_ref, h_ref, g_ref, b_ref, o_ref, *, alpha):
    y = alpha * h_ref[...] + _bdot(x_ref[...], w_ref[...])
    o_ref[...] = _layer_norm(y, g_ref[...], b_ref[...])


def _out_ln(x_bf, w_bf, h, ln_g, ln_b, alpha, tm):
    t, k = x_bf.shape
    d = w_bf.shape[1]
    tm = min(tm, t)
    assert t % tm == 0
    return pl.pallas_call(
        functools.partial(_out_ln_kernel, alpha=alpha),
        grid=(t // tm,),
        in_specs=[pl.BlockSpec((tm, k), lambda i: (i, 0)),
                  pl.BlockSpec((k, d), lambda i: (0, 0), pipeline_mode=pl.Buffered(1)),
                  pl.BlockSpec((tm, d), lambda i: (i, 0)),
                  pl.BlockSpec((1, d), lambda i: (0, 0)),
                  pl.BlockSpec((1, d), lambda i: (0, 0))],
        out_specs=pl.BlockSpec((tm, d), lambda i: (i, 0)),
        out_shape=jax.ShapeDtypeStruct((t, d), F32),
        compiler_params=_cparams(1),
        name="out_ln",
    )(x_bf, w_bf, h, ln_g.reshape(1, d), ln_b.reshape(1, d))


def _router_kernel(h_ref, wt_ref, rb_ref, e_ref, g_ref, p_ref, cnt_ref, carry_ref):
    n_e, tm = wt_ref.shape[0], h_ref.shape[0]

    @pl.when(pl.program_id(0) == 0)
    def _():
        carry_ref[...] = jnp.zeros_like(carry_ref)

    x = h_ref[...]
    xh = x.astype(BF16)
    xl = (x - xh.astype(F32)).astype(BF16)
    w = wt_ref[...]
    wh = w.astype(BF16)
    wl = (w - wh.astype(F32)).astype(BF16)
    dg = lambda a, b: lax.dot_general(a, b, NT_DIMS, preferred_element_type=F32)
    logits = dg(wh, xh) + dg(wh, xl) + dg(wl, xh) + rb_ref[...]

    eidx = lax.broadcasted_iota(I32, (n_e, tm), 0)
    work = logits
    tops, sel = [], []
    for _ in range(TOP_K):
        m = jnp.max(work, axis=0, keepdims=True)
        e = jnp.min(jnp.where(work == m, eidx, n_e), axis=0, keepdims=True)
        hit = eidx == e
        tops.append(m)
        sel.append(hit)
        e_ref[len(sel) - 1:len(sel), :] = e
        work = jnp.where(hit, -jnp.inf, work)

    ps = [jnp.exp(m - tops[0]) for m in tops]
    denom = ps[0] + ps[1] + ps[2] + ps[3]
    for k in range(TOP_K):
        g_ref[k:k + 1, :] = ps[k] / denom

    onehot = jnp.where(sel[0] | sel[1] | sel[2] | sel[3], 1.0, 0.0)
    before = (lax.broadcasted_iota(I32, (tm, tm), 0)
              < lax.broadcasted_iota(I32, (tm, tm), 1)).astype(BF16)
    rank = _bdot(onehot.astype(BF16), before) + carry_ref[:, 0:1]
    for k in range(TOP_K):
        pk = jnp.sum(jnp.where(sel[k], rank, 0.0), axis=0, keepdims=True)
        p_ref[k:k + 1, :] = pk.astype(I32)
    carry_ref[...] = carry_ref[...] + jnp.sum(onehot, axis=1, keepdims=True)
    cnt_ref[...] = carry_ref[...]


def _router(h, router_w, router_b, tm):
    t, d = h.shape
    n_e = router_w.shape[1]
    tm = min(tm, t)
    assert t % tm == 0
    tok = lambda i: (0, i)
    return pl.pallas_call(
        _router_kernel,
        grid=(t // tm,),
        in_specs=[pl.BlockSpec((tm, d), lambda i: (i, 0)),
                  pl.BlockSpec((n_e, d), lambda i: (0, 0)),
                  pl.BlockSpec((n_e, 1), lambda i: (0, 0))],
        out_specs=[pl.BlockSpec((TOP_K, tm), tok),
                   pl.BlockSpec((TOP_K, tm), tok),
                   pl.BlockSpec((TOP_K, tm), tok),
                   pl.BlockSpec((n_e, LANES), lambda i: (0, 0))],
        out_shape=[jax.ShapeDtypeStruct((TOP_K, t), I32),
                   jax.ShapeDtypeStruct((TOP_K, t), F32),
                   jax.ShapeDtypeStruct((TOP_K, t), I32),
                   jax.ShapeDtypeStruct((n_e, LANES), F32)],
        scratch_shapes=[pltpu.VMEM((n_e, LANES), F32)],
        compiler_params=_cparams(1),
        name="router",
    )(h, router_w.T, router_b.reshape(n_e, 1))


def _push_kernel(dest_ref, pad_lo_ref, pad_hi_ref, h_ref, xs_ref, pk_ref, z_ref, sem, zsem,
                 *, tm, n_pad):
    i = pl.program_id(0)
    pk_ref[...] = _pack_rows(h_ref[...])

    def row_copy(t, k):
        return pltpu.make_async_copy(pk_ref.at[pl.ds(t, 1)],
                                     xs_ref.at[pl.ds(dest_ref[0, k, t], 1)], sem)

    def issue(j, c):
        t0 = pl.multiple_of(j * PUSH_GROUP, PUSH_GROUP)
        for u in range(PUSH_GROUP):
            for k in range(TOP_K):
                row_copy(t0 + u, k).start()
        return c

    lax.fori_loop(0, tm // PUSH_GROUP, issue, 0)

    for k in range(TOP_K):
        pltpu.make_async_copy(pk_ref, xs_ref.at[pl.ds(0, tm)], sem).wait()

    @pl.when(i == pl.num_programs(0) - 1)
    def _():
        z_ref[...] = jnp.zeros_like(z_ref)

        def zero_copy(r):
            return pltpu.make_async_copy(z_ref.at[pl.ds(0, 1)], xs_ref.at[pl.ds(r, 1)], zsem)

        for e in range(n_pad):
            lo, hi = pad_lo_ref[e], pad_hi_ref[e]
            lax.fori_loop(lo, hi, lambda r, c: (zero_copy(r).start(), c)[1], 0)
            lax.fori_loop(lo, hi, lambda r, c: (zero_copy(r).wait(), c)[1], 0)


def _push(h, dest3, pad_lo, pad_hi, n_rows):
    t, d = h.shape
    n_steps, _, tm = dest3.shape
    return pl.pallas_call(
        functools.partial(_push_kernel, tm=tm, n_pad=pad_lo.shape[0]),
        grid=(n_steps,),
        in_specs=[pl.BlockSpec((1, TOP_K, tm), lambda i: (i, 0, 0), memory_space=pltpu.SMEM),
                  pl.BlockSpec(memory_space=pltpu.SMEM),
                  pl.BlockSpec(memory_space=pltpu.SMEM),
                  pl.BlockSpec((tm, d), lambda i: (i, 0))],
        out_specs=pl.BlockSpec(memory_space=pl.ANY),
        out_shape=jax.ShapeDtypeStruct((n_rows, d // 2), U32),
        scratch_shapes=[pltpu.VMEM((tm, d // 2), U32),
                        pltpu.VMEM((SUBLANES, d // 2), U32),
                        pltpu.SemaphoreType.DMA, pltpu.SemaphoreType.DMA],
        compiler_params=_cparams(1),
        name="push",
    )(dest3, pad_lo, pad_hi, h)


def _expert_kernel(te_ref, ts_ref, nu_ref, first_ref, nxt_ref, x_ref, wgu_hbm, bgu_ref, wd_hbm,
                   bd_ref, o_ref, wgu_f32, wd_f32, wgu_ref, wd_ref, sem, *, ff, layer):
    i = pl.program_id(0)
    used = i < nu_ref[0]

    def fetch(e):
        return (pltpu.make_async_copy(wgu_hbm.at[layer, e], wgu_f32, sem.at[0]),
                pltpu.make_async_copy(wd_hbm.at[layer, e], wd_f32, sem.at[1]))

    @pl.when(i == 0)
    def _():
        for c in fetch(te_ref[0]):
            c.start()

    @pl.when(first_ref[i] == 1)
    def _():
        for c in fetch(te_ref[i]):
            c.wait()
        wgu_ref[...] = wgu_f32[...].astype(BF16)
        wd_ref[...] = wd_f32[...].astype(BF16)

        @pl.when(nxt_ref[i] >= 0)
        def _():
            for c in fetch(nxt_ref[i]):
                c.start()

    @pl.when(used)
    def _():
        x_hi, x_lo = _unpack_rows(x_ref[...])
        x = jnp.concatenate([x_hi.astype(BF16), x_lo.astype(BF16)], axis=1)
        piece = x.shape[0] // EXPERT_GU_PIECES
        gu = jnp.concatenate([_bdot(x[p * piece:(p + 1) * piece], wgu_ref[...])
                              for p in range(EXPERT_GU_PIECES)], axis=0) + bgu_ref[...]
        hg = jnp.minimum(gu[:, :ff], SWIGLU_LIMIT)
        hu = jnp.clip(gu[:, ff:], -SWIGLU_LIMIT, SWIGLU_LIMIT)
        hdn = hg * jax.nn.sigmoid(SWIGLU_ALPHA * hg) * (hu + 1.0)
        o_ref[...] = _pack_rows(_bdot(hdn.astype(BF16), wd_ref[...]) + bd_ref[...])

    @pl.when(jnp.logical_not(used))
    def _():
        o_ref[...] = jnp.zeros_like(o_ref)


def _experts(xs, tile_e, tile_src, n_used, first, nxt, layer, w_gu, b_gu, w_down, b_down, tr):
    n_rows, d2 = xs.shape
    _, n_e, d, ff2 = w_gu.shape
    ff = ff2 // 2
    n_tiles = n_rows // tr
    grid_spec = pltpu.PrefetchScalarGridSpec(
        num_scalar_prefetch=5,
        grid=(n_tiles,),
        in_specs=[pl.BlockSpec((tr, d2), lambda i, te, ts, *_: (ts[i], 0)),
                  pl.BlockSpec(memory_space=pl.ANY),
                  pl.BlockSpec((None, 1, ff2), lambda i, te, *_: (te[i], 0, 0)),
                  pl.BlockSpec(memory_space=pl.ANY),
                  pl.BlockSpec((None, 1, d), lambda i, te, *_: (te[i], 0, 0))],
        out_specs=pl.BlockSpec((tr, d2), lambda i, *_: (i, 0)),
        scratch_shapes=[pltpu.VMEM((d, ff2), F32), pltpu.VMEM((ff, d), F32),
                        pltpu.VMEM((d, ff2), BF16), pltpu.VMEM((ff, d), BF16),
                        pltpu.SemaphoreType.DMA((2,))],
    )
    return pl.pallas_call(
        functools.partial(_expert_kernel, ff=ff, layer=layer),
        grid_spec=grid_spec,
        out_shape=jax.ShapeDtypeStruct((n_rows, d2), U32),
        compiler_params=_cparams(1),
        name="experts",
    )(tile_e, tile_src, n_used, first, nxt, xs, w_gu, b_gu.reshape(n_e, 1, ff2), w_down,
      b_down.reshape(n_e, 1, d))


def _combine_kernel(dcur_ref, dnext_ref, ys_ref, h_ref, gate_ref, g_ref, b_ref,
                    o_ref, obf_ref, buf_ref, sem, *, alpha, tm):
    i = pl.program_id(0)
    n = pl.num_programs(0)
    slot = i % 2
    d2 = buf_ref.shape[-1]
    grp = COMBINE_GROUP

    def start_group(dref, s, r0):
        for u in range(grp):
            for k in range(TOP_K):
                pltpu.make_async_copy(ys_ref.at[pl.ds(dref[0, k, r0 + u], 1)],
                                      buf_ref.at[s, k, pl.ds(r0 + u, 1)], sem.at[s]).start()

    def wait_slot(s):
        for k in range(TOP_K):
            pltpu.make_async_copy(ys_ref.at[pl.ds(0, tm)], buf_ref.at[s, k], sem.at[s]).wait()

    @pl.when(i == 0)
    def _():
        lax.fori_loop(0, tm // grp,
                      lambda j, c: (start_group(dcur_ref, 0, pl.multiple_of(j * grp, grp)), c)[1], 0)

    wait_slot(slot)

    def body(j, c):
        r0 = pl.multiple_of(j * grp, grp)
        rows = pl.ds(r0, grp)
        gate = gate_ref[rows, :]
        y_hi = alpha * h_ref[rows, :d2]
        y_lo = alpha * h_ref[rows, d2:]
        for k in range(TOP_K):
            e_hi, e_lo = _unpack_rows(buf_ref[slot, k, rows, :])
            y_hi = y_hi + gate[:, k:k + 1] * e_hi
            y_lo = y_lo + gate[:, k:k + 1] * e_lo
        out = _layer_norm(jnp.concatenate([y_hi, y_lo], axis=1), g_ref[...], b_ref[...])
        o_ref[rows, :] = out
        obf_ref[rows, :] = out.astype(obf_ref.dtype)
        start_group(dnext_ref, 1 - slot, r0)
        return c

    lax.fori_loop(0, tm // grp, body, 0)

    @pl.when(i == n - 1)
    def _():
        wait_slot(1 - slot)


def _combine(ys, dest3, h, gate_t, ln_g, ln_b, alpha):
    t, d = h.shape
    n_steps, _, tm = dest3.shape
    smem = lambda f: pl.BlockSpec((1, TOP_K, tm), f, memory_space=pltpu.SMEM)
    return pl.pallas_call(
        functools.partial(_combine_kernel, alpha=alpha, tm=tm),
        grid=(n_steps,),
        in_specs=[smem(lambda i: (i, 0, 0)),
                  smem(lambda i: (jnp.minimum(i + 1, n_steps - 1), 0, 0)),
                  pl.BlockSpec(memory_space=pl.ANY),
                  pl.BlockSpec((tm, d), lambda i: (i, 0)),
                  pl.BlockSpec((tm, TOP_K), lambda i: (i, 0)),
                  pl.BlockSpec((1, d), lambda i: (0, 0)),
                  pl.BlockSpec((1, d), lambda i: (0, 0))],
        out_specs=[pl.BlockSpec((tm, d), lambda i: (i, 0)),
                   pl.BlockSpec((tm, d), lambda i: (i, 0))],
        out_shape=[jax.ShapeDtypeStruct((t, d), F32),
                   jax.ShapeDtypeStruct((t, d), BF16)],
        scratch_shapes=[pltpu.VMEM((2, TOP_K, tm, d // 2), U32),
                        pltpu.SemaphoreType.DMA((2,))],
        compiler_params=_cparams(1),
        name="combine",
    )(dest3, dest3, ys, h, gate_t, ln_g.reshape(1, d), ln_b.reshape(1, d))


PROJ_TM, PROJ_TN = 1024, 1024
PROJ_CHUNKS = 4
SGU_ROWS = 512
RGLRU_TT, RGLRU_HEADS = 512, 2
OUT_TM = 128
ROUTER_TM = 512
PUSH_TM = 512
EXPERT_ROWS = 256
EXPERT_GU_PIECES = 2
COMBINE_TM = 256
COMBINE_GROUP = 16
PUSH_GROUP = 8


def _blocked_index(a, tm):
    k, t = a.shape
    return a.reshape(k, t // tm, tm).transpose(1, 0, 2)


def _moe(h, alpha, layer, router_w, router_b, w_gu, b_gu, w_down, b_down, ln_g, ln_b):
    t, d = h.shape
    n_e = router_w.shape[1]
    tr = EXPERT_ROWS
    top_e, gate, pos, cnt = _router(h, router_w, router_b, ROUTER_TM)

    counts = cnt[:, 0].astype(I32)
    padded = (counts + tr - 1) // tr * tr
    pad_end = jnp.cumsum(padded)
    pad_start = pad_end - padded
    expert_id = jnp.arange(n_e, dtype=I32)[:, None, None]
    dest = pos + jnp.sum(jnp.where(top_e[None] == expert_id, pad_start[:, None, None], 0), axis=0)
    n_tiles = (t * TOP_K + n_e * (tr - 1)) // tr + 1
    n_used = (pad_end[-1] // tr).astype(I32)
    tile_id = jnp.arange(n_tiles, dtype=I32)
    tile_e = jnp.sum((pad_end[None, :] <= (tile_id * tr)[:, None]).astype(I32), axis=1)
    tile_e = jnp.minimum(tile_e, n_e - 1)
    tile_src = jnp.minimum(tile_id, n_used - 1)
    prev_e = jnp.concatenate([jnp.full((1,), -1, I32), tile_e[:-1]])
    first = ((tile_id < n_used) & (tile_e != prev_e)).astype(I32)
    run_end = pad_end[tile_e] // tr
    nxt = jnp.where(run_end < n_used, tile_e[jnp.minimum(run_end, n_tiles - 1)], -1).astype(I32)

    n_rows = n_tiles * tr
    pad_lo = jnp.concatenate([pad_start + counts, pad_end[-1:]]).astype(I32)
    pad_hi = jnp.concatenate([pad_end, jnp.full((1,), n_rows)]).astype(I32)
    xs = _push(h, _blocked_index(dest, min(PUSH_TM, t)), pad_lo, pad_hi, n_rows)
    ys = _experts(xs, tile_e, tile_src, n_used.reshape(1), first, nxt, layer,
                  w_gu, b_gu, w_down, b_down, tr)
    return _combine(ys, _blocked_index(dest, min(COMBINE_TM, t)), h, gate.T, ln_g, ln_b, alpha)


def kernel(x, a_w_in, a_ln_g, a_ln_b, a_w_s, a_b_s, a_w_out, b_w_in, b_conv_w, b_conv_b, b_w_r, b_b_r, b_w_i, b_b_i, b_lambda, b_w_out, ln1_g, ln1_b, ln2_g, ln2_b, router_w, router_b, ex_w_gu, ex_b_gu, ex_w_down, ex_b_down):
    bsz, seq, d = x.shape
    depth = ln1_g.shape[0]
    alpha = (2 * depth) ** 0.25
    h = x.reshape(bsz * seq, d)
    h_bf = h.astype(BF16)
    for layer in range(depth):
        j = layer // 2
        if layer % 2 == 0:
            z = _proj(h_bf, a_w_in, j, 0, 2 * d, True, PROJ_TM, PROJ_TN)
            mixed = _sgu(z, a_ln_g[j], a_ln_b[j], a_w_s[j], a_b_s[j], SGU_ROWS)
            w_out = a_w_out[j]
        else:
            y = _proj(h_bf, b_w_in, j, 0, d, True, PROJ_TM, PROJ_TN)
            xr = _proj(h_bf, b_w_in, j, d, d, False, PROJ_TM, PROJ_TN)
            mixed = _rglru(y, xr, bsz, b_conv_w[j], b_conv_b[j], b_w_r[j], b_b_r[j],
                           b_w_i[j], b_b_i[j], b_lambda[j], RGLRU_TT, RGLRU_HEADS)
            w_out = b_w_out[j]
        h1 = _out_ln(mixed, w_out.astype(BF16), h, ln1_g[layer], ln1_b[layer], alpha, OUT_TM)
        h, h_bf = _moe(h1, alpha, layer, router_w[layer], router_b[layer],
                       ex_w_gu, ex_b_gu[layer], ex_w_down, ex_b_down[layer],
                       ln2_g[layer], ln2_b[layer])
    return h.reshape(bsz, seq, d)
```

```python
import functools

import jax
import jax.numpy as jnp
from jax import lax
from jax.experimental import pallas as pl
from jax.experimental.pallas import tpu as pltpu

F32 = jnp.float32
BF16 = jnp.bfloat16
I32 = jnp.int32
U32 = jnp.uint32

CHUNK = 128
N_HEADS = 16
CONV_WIDTH = 4
LRU_C = 8.0
TOP_K = 4
SWIGLU_LIMIT = 7.0
SWIGLU_ALPHA = 1.702
LN_EPS = 1e-5

LANES = 128
SUBLANES = 8
VMEM_LIMIT_BYTES = 60000 * 1024

NT_DIMS = (((1,), (1,)), ((), ()))


def _cparams(n_axes, vmem=VMEM_LIMIT_BYTES):
    return pltpu.CompilerParams(
        dimension_semantics=("arbitrary",) * n_axes, vmem_limit_bytes=vmem)


def _gelu(x):
    return 0.5 * x * (1.0 + jnp.tanh(0.7978845608028654 * (x + 0.044715 * (x * x * x))))


def _layer_norm(x, g, b):
    mu = jnp.mean(x, axis=-1, keepdims=True)
    xc = x - mu
    var = jnp.mean(xc * xc, axis=-1, keepdims=True)
    return xc * lax.rsqrt(var + LN_EPS) * g + b


def _bdot(a, b):
    return jnp.dot(a, b, preferred_element_type=F32)


def _pack_rows(x):
    n2 = x.shape[1] // 2
    hi = lax.bitcast_convert_type(x[:, :n2].astype(BF16).astype(F32), U32)
    lo = lax.bitcast_convert_type(x[:, n2:].astype(BF16).astype(F32), U32)
    return hi | (lo >> 16)


def _unpack_rows(p):
    hi = lax.bitcast_convert_type(p & jnp.uint32(0xFFFF0000), F32)
    lo = lax.bitcast_convert_type(p << 16, F32)
    return hi, lo


def _proj_kernel(x_ref, w_ref, o_ref, wbf_ref, *, gelu, n_chunks):
    @pl.when(pl.program_id(1) == 0)
    def _():
        wbf_ref[...] = w_ref[...].astype(BF16)

    rows_per_chunk = x_ref.shape[0] // n_chunks
    for c in range(n_chunks):
        rows = slice(c * rows_per_chunk, (c + 1) * rows_per_chunk)
        acc = _bdot(x_ref[rows, :], wbf_ref[...])
        o_ref[rows, :] = (_gelu(acc) if gelu else acc).astype(o_ref.dtype)


def _proj(x_bf, w, layer, col0, n, gelu, tm, tn):
    t, k = x_bf.shape
    tm, tn = min(tm, t), min(tn, n)
    assert t % tm == 0 and n % tn == 0 and col0 % tn == 0
    return pl.pallas_call(
        functools.partial(_proj_kernel, gelu=gelu, n_chunks=PROJ_CHUNKS),
        grid=(n // tn, t // tm),
        in_specs=[pl.BlockSpec((tm, k), lambda j, i: (i, 0)),
                  pl.BlockSpec((None, k, tn), lambda j, i: (layer, 0, col0 // tn + j),
                               pipeline_mode=pl.Buffered(1))],
        out_specs=pl.BlockSpec((tm, tn), lambda j, i: (i, j)),
        out_shape=jax.ShapeDtypeStruct((t, n), BF16),
        scratch_shapes=[pltpu.VMEM((k, tn), BF16)],
        compiler_params=_cparams(2),
        name="proj",
    )(x_bf, w)


def _sgu_kernel(u_ref, v_ref, g_ref, b_ref, ws_ref, bs_ref, o_ref, *, n_chunks, hd):
    row = lax.broadcasted_iota(I32, (CHUNK, CHUNK), 0)
    col = lax.broadcasted_iota(I32, (CHUNK, CHUNK), 1)
    causal = row >= col
    for c in range(n_chunks):
        rows = slice(c * CHUNK, (c + 1) * CHUNK)
        vn = _layer_norm(v_ref[rows, :].astype(F32), g_ref[...], b_ref[...]).astype(BF16)
        for h in range(N_HEADS):
            cols = slice(h * hd, (h + 1) * hd)
            wm = jnp.where(causal, ws_ref[h], 0.0).astype(BF16)
            sv = _bdot(wm, vn[:, cols]) + bs_ref[:, h:h + 1]
            o_ref[rows, cols] = (u_ref[rows, cols].astype(F32) * sv).astype(o_ref.dtype)


def _sgu(z_bf, ln_g, ln_b, w_s, b_s, rows_per_step):
    t, w2 = z_bf.shape
    w = w2 // 2
    hd = w // N_HEADS
    r = min(rows_per_step, t)
    assert t % r == 0 and r % CHUNK == 0
    return pl.pallas_call(
        functools.partial(_sgu_kernel, n_chunks=r // CHUNK, hd=hd),
        grid=(t // r,),
        in_specs=[pl.BlockSpec((r, w), lambda i: (i, 0)),
                  pl.BlockSpec((r, w), lambda i: (i, 1)),
                  pl.BlockSpec((1, w), lambda i: (0, 0)),
                  pl.BlockSpec((1, w), lambda i: (0, 0)),
                  pl.BlockSpec((N_HEADS, CHUNK, CHUNK), lambda i: (0, 0, 0)),
                  pl.BlockSpec((CHUNK, N_HEADS), lambda i: (0, 0))],
        out_specs=pl.BlockSpec((r, w), lambda i: (i, 0)),
        out_shape=jax.ShapeDtypeStruct((t, w), BF16),
        compiler_params=_cparams(1),
        name="sgu",
    )(z_bf, z_bf, ln_g.reshape(1, w), ln_b.reshape(1, w), w_s, b_s.T)


def _rglru_kernel(y_ref, x_ref, cw_ref, cb_ref, wr_ref, br_ref, wi_ref, bi_ref, lam_ref,
                  o_ref, tail_ref, hc_ref, hs_ref, *, heads_per_step, hd):
    tt, cg = x_ref.shape
    n8 = tt // SUBLANES

    @pl.when(pl.program_id(2) == 0)
    def _():
        tail_ref[...] = jnp.zeros_like(tail_ref)
        hc_ref[...] = jnp.zeros_like(hc_ref)

    sub = lax.broadcasted_iota(I32, (1, SUBLANES, 1), 1)
    xr = x_ref[...].astype(F32)
    x3 = xr.reshape(n8, SUBLANES, cg)
    tail3 = tail_ref[...].reshape(1, SUBLANES, cg)
    xc3 = x3 * cw_ref[CONV_WIDTH - 1:CONV_WIDTH, :] + cb_ref[...]
    for s in range(1, CONV_WIDTH):
        k = CONV_WIDTH - 1 - s
        rot = pltpu.roll(x3, s, 1)
        prev = jnp.concatenate([pltpu.roll(tail3, s, 1), rot[:n8 - 1]], axis=0)
        xc3 = xc3 + jnp.where(sub < s, prev, rot) * cw_ref[k:k + 1, :]
    tail_ref[...] = xr[tt - SUBLANES:, :]
    xc = xc3.reshape(tt, cg)

    xcb = xc.astype(BF16)
    r_parts, i_parts = [], []
    for h in range(heads_per_step):
        cols = slice(h * hd, (h + 1) * hd)
        r_parts.append(_bdot(xcb[:, cols], wr_ref[h].astype(BF16)))
        i_parts.append(_bdot(xcb[:, cols], wi_ref[h].astype(BF16)))
    r = jax.nn.sigmoid(jnp.concatenate(r_parts, axis=1) + br_ref[...])
    ig = jax.nn.sigmoid(jnp.concatenate(i_parts, axis=1) + bi_ref[...])

    lam = lam_ref[...]
    softplus_neg_lam = jnp.maximum(-lam, 0.0) + jnp.log1p(jnp.exp(-jnp.abs(lam)))
    log_a = (-LRU_C * r) * softplus_neg_lam
    a = jnp.exp(log_a)
    th = jnp.tanh(log_a)
    mult = jnp.sqrt(-2.0 * th / (1.0 - th))
    b = mult * (ig * xc)

    a = a.reshape(n8, SUBLANES, cg)
    b = b.reshape(n8, SUBLANES, cg)
    for d in (1, 2, 4):
        keep = sub >= d
        a_prev = jnp.where(keep, pltpu.roll(a, d, 1), 1.0)
        b_prev = jnp.where(keep, pltpu.roll(b, d, 1), 0.0)
        b = a * b_prev + b
        a = a * a_prev

    carry = hc_ref[...]
    for j in range(n8):
        hj = a[j] * carry + b[j]
        hs_ref[j * SUBLANES:(j + 1) * SUBLANES, :] = hj
        carry = jnp.broadcast_to(hj[SUBLANES - 1:SUBLANES, :], hj.shape)
    hc_ref[...] = carry

    o_ref[...] = (hs_ref[...] * y_ref[...].astype(F32)).astype(o_ref.dtype)


def _rglru(y_bf, x_bf, bsz, conv_w, conv_b, w_r, b_r, w_i, b_i, lam, tt, heads_per_step):
    t, w = x_bf.shape
    hd = w // N_HEADS
    seq = t // bsz
    tt = min(tt, seq)
    hps = heads_per_step
    cg = hps * hd
    assert seq % tt == 0 and N_HEADS % hps == 0 and tt % SUBLANES == 0
    n_g, n_s = N_HEADS // hps, seq // tt
    row = lambda g, b, s: b * n_s + s
    vec = lambda g, b, s: (0, g)
    return pl.pallas_call(
        functools.partial(_rglru_kernel, heads_per_step=hps, hd=hd),
        grid=(n_g, bsz, n_s),
        in_specs=[pl.BlockSpec((tt, cg), lambda g, b, s: (row(g, b, s), g)),
                  pl.BlockSpec((tt, cg), lambda g, b, s: (row(g, b, s), g)),
                  pl.BlockSpec((CONV_WIDTH, cg), vec),
                  pl.BlockSpec((1, cg), vec),
                  pl.BlockSpec((hps, hd, hd), lambda g, b, s: (g, 0, 0)),
                  pl.BlockSpec((1, cg), vec),
                  pl.BlockSpec((hps, hd, hd), lambda g, b, s: (g, 0, 0)),
                  pl.BlockSpec((1, cg), vec),
                  pl.BlockSpec((1, cg), vec)],
        out_specs=pl.BlockSpec((tt, cg), lambda g, b, s: (row(g, b, s), g)),
        out_shape=jax.ShapeDtypeStruct((t, w), BF16),
        scratch_shapes=[pltpu.VMEM((SUBLANES, cg), F32),
                        pltpu.VMEM((SUBLANES, cg), F32),
                        pltpu.VMEM((tt, cg), F32)],
        compiler_params=_cparams(3),
        name="rglru",
    )(y_bf, x_bf, conv_w, conv_b.reshape(1, w), w_r, b_r.reshape(1, w),
      w_i, b_i.reshape(1, w), lam.reshape(1, w))


def _out_ln_kernel(x_ref, w_ref, h_ref, g_ref, b_ref, o_ref, *, alpha):
    y = alpha * h_ref[...] + _bdot(x_ref[...], w_ref[...])
    o_ref[...] = _layer_norm(y, g_ref[...], b_ref[...])


def _out_ln(x_bf, w_bf, h, ln_g, ln_b, alpha, tm):
    t, k = x_bf.shape
    d = w_bf.shape[1]
    tm = min(tm, t)
    assert t % tm == 0
    return pl.pallas_call(
        functools.partial(_out_ln_kernel, alpha=alpha),
        grid=(t // tm,),
        in_specs=[pl.BlockSpec((tm, k), lambda i: (i, 0)),
                  pl.BlockSpec((k, d), lambda i: (0, 0), pipeline_mode=pl.Buffered(1)),
                  pl.BlockSpec((tm, d), lambda i: (i, 0)),
                  pl.BlockSpec((1, d), lambda i: (0, 0)),
                  pl.BlockSpec((1, d), lambda i: (0, 0))],
        out_specs=pl.BlockSpec((tm, d), lambda i: (i, 0)),
        out_shape=jax.ShapeDtypeStruct((t, d), F32),
        compiler_params=_cparams(1),
        name="out_ln",
    )(x_bf, w_bf, h, ln_g.reshape(1, d), ln_b.reshape(1, d))


def _router_kernel(h_ref, wt_ref, rb_ref, e_ref, g_ref, p_ref, cnt_ref, carry_ref):
    n_e, tm = wt_ref.shape[0], h_ref.shape[0]

    @pl.when(pl.program_id(0) == 0)
    def _():
        carry_ref[...] = jnp.zeros_like(carry_ref)

    x = h_ref[...]
    xh = x.astype(BF16)
    xl = (x - xh.astype(F32)).astype(BF16)
    w = wt_ref[...]
    wh = w.astype(BF16)
    wl = (w - wh.astype(F32)).astype(BF16)
    dg = lambda a, b: lax.dot_general(a, b, NT_DIMS, preferred_element_type=F32)
    logits = dg(wh, xh) + dg(wh, xl) + dg(wl, xh) + rb_ref[...]

    eidx = lax.broadcasted_iota(I32, (n_e, tm), 0)
    work = logits
    tops, sel = [], []
    for _ in range(TOP_K):
        m = jnp.max(work, axis=0, keepdims=True)
        e = jnp.min(jnp.where(work == m, eidx, n_e), axis=0, keepdims=True)
        hit = eidx == e
        tops.append(m)
        sel.append(hit)
        e_ref[len(sel) - 1:len(sel), :] = e
        work = jnp.where(hit, -jnp.inf, work)

    ps = [jnp.exp(m - tops[0]) for m in tops]
    denom = ps[0] + ps[1] + ps[2] + ps[3]
    for k in range(TOP_K):
        g_ref[k:k + 1, :] = ps[k] / denom

    onehot = jnp.where(sel[0] | sel[1] | sel[2] | sel[3], 1.0, 0.0)
    before = (lax.broadcasted_iota(I32, (tm, tm), 0)
              < lax.broadcasted_iota(I32, (tm, tm), 1)).astype(BF16)
    rank = _bdot(onehot.astype(BF16), before) + carry_ref[:, 0:1]
    for k in range(TOP_K):
        pk = jnp.sum(jnp.where(sel[k], rank, 0.0), axis=0, keepdims=True)
        p_ref[k:k + 1, :] = pk.astype(I32)
    carry_ref[...] = carry_ref[...] + jnp.sum(onehot, axis=1, keepdims=True)
    cnt_ref[...] = carry_ref[...]


def _router(h, router_w, router_b, tm):
    t, d = h.shape
    n_e = router_w.shape[1]
    tm = min(tm, t)
    assert t % tm == 0
    tok = lambda i: (0, i)
    return pl.pallas_call(
        _router_kernel,
        grid=(t // tm,),
        in_specs=[pl.BlockSpec((tm, d), lambda i: (i, 0)),
                  pl.BlockSpec((n_e, d), lambda i: (0, 0)),
                  pl.BlockSpec((n_e, 1), lambda i: (0, 0))],
        out_specs=[pl.BlockSpec((TOP_K, tm), tok),
                   pl.BlockSpec((TOP_K, tm), tok),
                   pl.BlockSpec((TOP_K, tm), tok),
                   pl.BlockSpec((n_e, LANES), lambda i: (0, 0))],
        out_shape=[jax.ShapeDtypeStruct((TOP_K, t), I32),
                   jax.ShapeDtypeStruct((TOP_K, t), F32),
                   jax.ShapeDtypeStruct((TOP_K, t), I32),
                   jax.ShapeDtypeStruct((n_e, LANES), F32)],
        scratch_shapes=[pltpu.VMEM((n_e, LANES), F32)],
        compiler_params=_cparams(1),
        name="router",
    )(h, router_w.T, router_b.reshape(n_e, 1))


def _push_kernel(dest_ref, pad_lo_ref, pad_hi_ref, h_ref, xs_ref, pk_ref, z_ref, sem, zsem,
                 *, tm, n_pad):
    i = pl.program_id(0)
    pk_ref[...] = _pack_rows(h_ref[...]).reshape(pk_ref.shape)

    def issue(j, c):
        for u in range(SUBLANES):
            for k in range(TOP_K):
                dst = dest_ref[0, 0, (j * SUBLANES + u) * TOP_K + k]
                pltpu.make_async_copy(pk_ref.at[j, pl.ds(u, 1)], xs_ref.at[pl.ds(dst, 1)], sem).start()
        return c

    lax.fori_loop(0, tm // SUBLANES, issue, 0)

    for _ in range(SUBLANES * TOP_K):
        pltpu.make_async_copy(pk_ref.at[:, 0], xs_ref.at[pl.ds(0, tm // SUBLANES)], sem).wait()

    @pl.when(i == pl.num_programs(0) - 1)
    def _():
        z_ref[...] = jnp.zeros_like(z_ref)

        def zero_copy(r):
            return pltpu.make_async_copy(z_ref.at[pl.ds(0, 1)], xs_ref.at[pl.ds(r, 1)], zsem)

        for e in range(n_pad):
            lo, hi = pad_lo_ref[e], pad_hi_ref[e]
            lax.fori_loop(lo, hi, lambda r, c: (zero_copy(r).start(), c)[1], 0)
            lax.fori_loop(lo, hi, lambda r, c: (zero_copy(r).wait(), c)[1], 0)


def _push(h, dest3, pad_lo, pad_hi, n_rows):
    t, d = h.shape
    n_steps, tm = dest3.shape[0], dest3.shape[2] // TOP_K
    return pl.pallas_call(
        functools.partial(_push_kernel, tm=tm, n_pad=pad_lo.shape[0]),
        grid=(n_steps,),
        in_specs=[pl.BlockSpec((1, 1, TOP_K * tm), lambda i: (i, 0, 0), memory_space=pltpu.SMEM),
                  pl.BlockSpec(memory_space=pltpu.SMEM),
                  pl.BlockSpec(memory_space=pltpu.SMEM),
                  pl.BlockSpec((tm, d), lambda i: (i, 0))],
        out_specs=pl.BlockSpec(memory_space=pl.ANY),
        out_shape=jax.ShapeDtypeStruct((n_rows, d // 2), U32),
        scratch_shapes=[pltpu.VMEM((tm // SUBLANES, SUBLANES, d // 2), U32),
                        pltpu.VMEM((SUBLANES, d // 2), U32),
                        pltpu.SemaphoreType.DMA, pltpu.SemaphoreType.DMA],
        compiler_params=_cparams(1),
        name="push",
    )(dest3, pad_lo, pad_hi, h)


def _expert_kernel(te_ref, ts_ref, nu_ref, first_ref, nxt_ref, x_ref, wgu_hbm, bgu_ref, wd_hbm,
                   bd_ref, o_ref, wgu_f32, wd_f32, wgu_ref, wd_ref, sem, *, ff, layer):
    i = pl.program_id(0)
    used = i < nu_ref[0]

    def fetch(e):
        return (pltpu.make_async_copy(wgu_hbm.at[layer, e], wgu_f32, sem.at[0]),
                pltpu.make_async_copy(wd_hbm.at[layer, e], wd_f32, sem.at[1]))

    @pl.when(i == 0)
    def _():
        for c in fetch(te_ref[0]):
            c.start()

    @pl.when(first_ref[i] == 1)
    def _():
        for c in fetch(te_ref[i]):
            c.wait()
        wgu_ref[...] = wgu_f32[...].astype(BF16)
        wd_ref[...] = wd_f32[...].astype(BF16)

        @pl.when(nxt_ref[i] >= 0)
        def _():
            for c in fetch(nxt_ref[i]):
                c.start()

    @pl.when(used)
    def _():
        x_hi, x_lo = _unpack_rows(x_ref[...])
        x = jnp.concatenate([x_hi.astype(BF16), x_lo.astype(BF16)], axis=1)
        piece = x.shape[0] // EXPERT_GU_PIECES
        gu = jnp.concatenate([_bdot(x[p * piece:(p + 1) * piece], wgu_ref[...])
                              for p in range(EXPERT_GU_PIECES)], axis=0) + bgu_ref[...]
        hg = jnp.minimum(gu[:, :ff], SWIGLU_LIMIT)
        hu = jnp.clip(gu[:, ff:], -SWIGLU_LIMIT, SWIGLU_LIMIT)
        hdn = hg * jax.nn.sigmoid(SWIGLU_ALPHA * hg) * (hu + 1.0)
        o_ref[...] = _pack_rows(_bdot(hdn.astype(BF16), wd_ref[...]) + bd_ref[...])

    @pl.when(jnp.logical_not(used))
    def _():
        o_ref[...] = jnp.zeros_like(o_ref)


def _experts(xs, tile_e, tile_src, n_used, first, nxt, layer, w_gu, b_gu, w_down, b_down, tr):
    n_rows, d2 = xs.shape
    _, n_e, d, ff2 = w_gu.shape
    ff = ff2 // 2
    n_tiles = n_rows // tr
    grid_spec = pltpu.PrefetchScalarGridSpec(
        num_scalar_prefetch=5,
        grid=(n_tiles,),
        in_specs=[pl.BlockSpec((tr, d2), lambda i, te, ts, *_: (ts[i], 0)),
                  pl.BlockSpec(memory_space=pl.ANY),
                  pl.BlockSpec((None, 1, ff2), lambda i, te, *_: (te[i], 0, 0)),
                  pl.BlockSpec(memory_space=pl.ANY),
                  pl.BlockSpec((None, 1, d), lambda i, te, *_: (te[i], 0, 0))],
        out_specs=pl.BlockSpec((tr, d2), lambda i, *_: (i, 0)),
        scratch_shapes=[pltpu.VMEM((d, ff2), F32), pltpu.VMEM((ff, d), F32),
                        pltpu.VMEM((d, ff2), BF16), pltpu.VMEM((ff, d), BF16),
                        pltpu.SemaphoreType.DMA((2,))],
    )
    return pl.pallas_call(
        functools.partial(_expert_kernel, ff=ff, layer=layer),
        grid_spec=grid_spec,
        out_shape=jax.ShapeDtypeStruct((n_rows, d2), U32),
        compiler_params=_cparams(1),
        name="experts",
    )(tile_e, tile_src, n_used, first, nxt, xs, w_gu, b_gu.reshape(n_e, 1, ff2), w_down,
      b_down.reshape(n_e, 1, d))


def _combine_kernel(dcur_ref, dnext_ref, ys_ref, h_ref, gate_ref, g_ref, b_ref,
                    o_ref, obf_ref, buf_ref, sem, *, alpha, tm):
    i = pl.program_id(0)
    n = pl.num_programs(0)
    slot = i % 2
    d2 = buf_ref.shape[-1]
    grp = COMBINE_GROUP

    gps = grp // SUBLANES

    def start_group(dref, s, j):
        for u in range(grp):
            for k in range(TOP_K):
                src = dref[0, 0, (j * grp + u) * TOP_K + k]
                pltpu.make_async_copy(
                    ys_ref.at[pl.ds(src, 1)],
                    buf_ref.at[s, k, j * gps + u // SUBLANES, pl.ds(u % SUBLANES, 1)],
                    sem.at[s]).start()

    def wait_slot(s):
        for k in range(TOP_K):
            for _ in range(SUBLANES):
                pltpu.make_async_copy(ys_ref.at[pl.ds(0, tm // SUBLANES)], buf_ref.at[s, k, :, 0],
                                      sem.at[s]).wait()

    @pl.when(i == 0)
    def _():
        lax.fori_loop(0, tm // grp, lambda j, c: (start_group(dcur_ref, 0, j), c)[1], 0)

    wait_slot(slot)

    def body(j, c):
        r0 = pl.multiple_of(j * grp, grp)
        rows = pl.ds(r0, grp)
        gate = gate_ref[rows, :]
        y_hi = alpha * h_ref[rows, :d2]
        y_lo = alpha * h_ref[rows, d2:]
        for k in range(TOP_K):
            e_hi, e_lo = _unpack_rows(buf_ref[slot, k, pl.ds(j * gps, gps)].reshape(grp, d2))
            y_hi = y_hi + gate[:, k:k + 1] * e_hi
            y_lo = y_lo + gate[:, k:k + 1] * e_lo
        out = _layer_norm(jnp.concatenate([y_hi, y_lo], axis=1), g_ref[...], b_ref[...])
        o_ref[rows, :] = out
        obf_ref[rows, :] = out.astype(obf_ref.dtype)
        start_group(dnext_ref, 1 - slot, j)
        return c

    lax.fori_loop(0, tm // grp, body, 0)

    @pl.when(i == n - 1)
    def _():
        wait_slot(1 - slot)


def _combine(ys, dest3, h, gate_t, ln_g, ln_b, alpha):
    t, d = h.shape
    n_steps, tm = dest3.shape[0], dest3.shape[2] // TOP_K
    smem = lambda f: pl.BlockSpec((1, 1, TOP_K * tm), f, memory_space=pltpu.SMEM)
    return pl.pallas_call(
        functools.partial(_combine_kernel, alpha=alpha, tm=tm),
        grid=(n_steps,),
        in_specs=[smem(lambda i: (i, 0, 0)),
                  smem(lambda i: (jnp.minimum(i + 1, n_steps - 1), 0, 0)),
                  pl.BlockSpec(memory_space=pl.ANY),
                  pl.BlockSpec((tm, d), lambda i: (i, 0)),
                  pl.BlockSpec((tm, TOP_K), lambda i: (i, 0)),
                  pl.BlockSpec((1, d), lambda i: (0, 0)),
                  pl.BlockSpec((1, d), lambda i: (0, 0))],
        out_specs=[pl.BlockSpec((tm, d), lambda i: (i, 0)),
                   pl.BlockSpec((tm, d), lambda i: (i, 0))],
        out_shape=[jax.ShapeDtypeStruct((t, d), F32),
                   jax.ShapeDtypeStruct((t, d), BF16)],
        scratch_shapes=[pltpu.VMEM((2, TOP_K, tm // SUBLANES, SUBLANES, d // 2), U32),
                        pltpu.SemaphoreType.DMA((2,))],
        compiler_params=_cparams(1),
        name="combine",
    )(dest3, dest3, ys, h, gate_t, ln_g.reshape(1, d), ln_b.reshape(1, d))


PROJ_TM, PROJ_TN = 1024, 1024
PROJ_CHUNKS = 4
SGU_ROWS = 512
RGLRU_TT, RGLRU_HEADS = 512, 2
OUT_TM = 128
ROUTER_TM = 512
PUSH_TM = 512
EXPERT_ROWS = 256
EXPERT_GU_PIECES = 2
COMBINE_TM = 256
COMBINE_GROUP = 16


def _blocked_index(a, tm):
    k, t = a.shape
    return a.T.reshape(t // tm, 1, tm * k)


def _moe(h, alpha, layer, router_w, router_b, w_gu, b_gu, w_down, b_down, ln_g, ln_b):
    t, d = h.shape
    n_e = router_w.shape[1]
    tr = EXPERT_ROWS
    top_e, gate, pos, cnt = _router(h, router_w, router_b, ROUTER_TM)

    counts = cnt[:, 0].astype(I32)
    padded = (counts + tr - 1) // tr * tr
    pad_end = jnp.cumsum(padded)
    pad_start = pad_end - padded
    expert_id = jnp.arange(n_e, dtype=I32)[:, None, None]
    dest = pos + jnp.sum(jnp.where(top_e[None] == expert_id, pad_start[:, None, None], 0), axis=0)
    n_tiles = (t * TOP_K + n_e * (tr - 1)) // tr + 1
    n_used = (pad_end[-1] // tr).astype(I32)
    tile_id = jnp.arange(n_tiles, dtype=I32)
    tile_e = jnp.sum((pad_end[None, :] <= (tile_id * tr)[:, None]).astype(I32), axis=1)
    tile_e = jnp.minimum(tile_e, n_e - 1)
    tile_src = jnp.minimum(tile_id, n_used - 1)
    prev_e = jnp.concatenate([jnp.full((1,), -1, I32), tile_e[:-1]])
    first = ((tile_id < n_used) & (tile_e != prev_e)).astype(I32)
    run_end = pad_end[tile_e] // tr
    nxt = jnp.where(run_end < n_used, tile_e[jnp.minimum(run_end, n_tiles - 1)], -1).astype(I32)

    n_rows = n_tiles * tr
    pad_lo = jnp.concatenate([pad_start + counts, pad_end[-1:]]).astype(I32)
    pad_hi = jnp.concatenate([pad_end, jnp.full((1,), n_rows)]).astype(I32)
    xs = _push(h, _blocked_index(dest, min(PUSH_TM, t)), pad_lo, pad_hi, n_rows)
    ys = _experts(xs, tile_e, tile_src, n_used.reshape(1), first, nxt, layer,
                  w_gu, b_gu, w_down, b_down, tr)
    return _combine(ys, _blocked_index(dest, min(COMBINE_TM, t)), h, gate.T, ln_g, ln_b, alpha)


def kernel(x, a_w_in, a_ln_g, a_ln_b, a_w_s, a_b_s, a_w_out, b_w_in, b_conv_w, b_conv_b, b_w_r, b_b_r, b_w_i, b_b_i, b_lambda, b_w_out, ln1_g, ln1_b, ln2_g, ln2_b, router_w, router_b, ex_w_gu, ex_b_gu, ex_w_down, ex_b_down):
    bsz, seq, d = x.shape
    depth = ln1_g.shape[0]
    alpha = (2 * depth) ** 0.25
    h = x.reshape(bsz * seq, d)
    h_bf = h.astype(BF16)
    for layer in range(depth):
        j = layer // 2
        if layer % 2 == 0:
            z = _proj(h_bf, a_w_in, j, 0, 2 * d, True, PROJ_TM, PROJ_TN)
            mixed = _sgu(z, a_ln_g[j], a_ln_b[j], a_w_s[j], a_b_s[j], SGU_ROWS)
            w_out = a_w_out[j]
        else:
            y = _proj(h_bf, b_w_in, j, 0, d, True, PROJ_TM, PROJ_TN)
            xr = _proj(h_bf, b_w_in, j, d, d, False, PROJ_TM, PROJ_TN)
            mixed = _rglru(y, xr, bsz, b_conv_w[j], b_conv_b[j], b_w_r[j], b_b_r[j],
                           b_w_i[j], b_b_i[j], b_lambda[j], RGLRU_TT, RGLRU_HEADS)
            w_out = b_w_out[j]
        h1 = _out_ln(mixed, w_out.astype(BF16), h, ln1_g[layer], ln1_b[layer], alpha, OUT_TM)
        h, h_bf = _moe(h1, alpha, layer, router_w[layer], router_b[layer],
                       ex_w_gu, ex_b_gu[layer], ex_w_down, ex_b_down[layer],
                       ln2_g[layer], ln2_b[layer])
    return h.reshape(bsz, seq, d)
```

```python
import functools

import jax
import jax.numpy as jnp
from jax import lax
from jax.experimental import pallas as pl
from jax.experimental.pallas import tpu as pltpu

F32 = jnp.float32
BF16 = jnp.bfloat16
I32 = jnp.int32
U32 = jnp.uint32

CHUNK = 128
N_HEADS = 16
CONV_WIDTH = 4
LRU_C = 8.0
TOP_K = 4
SWIGLU_LIMIT = 7.0
SWIGLU_ALPHA = 1.702
LN_EPS = 1e-5

LANES = 128
SUBLANES = 8
VMEM_LIMIT_BYTES = 60000 * 1024

NT_DIMS = (((1,), (1,)), ((), ()))


def _cparams(n_axes, vmem=VMEM_LIMIT_BYTES):
    return pltpu.CompilerParams(
        dimension_semantics=("arbitrary",) * n_axes, vmem_limit_bytes=vmem)


def _gelu(x):
    return 0.5 * x * (1.0 + jnp.tanh(0.7978845608028654 * (x + 0.044715 * (x * x * x))))


def _layer_norm(x, g, b):
    mu = jnp.mean(x, axis=-1, keepdims=True)
    xc = x - mu
    var = jnp.mean(xc * xc, axis=-1, keepdims=True)
    return xc * lax.rsqrt(var + LN_EPS) * g + b


def _bdot(a, b):
    return jnp.dot(a, b, preferred_element_type=F32)


def _pack_rows(x):
    n2 = x.shape[1] // 2
    hi = lax.bitcast_convert_type(x[:, :n2].astype(BF16).astype(F32), U32)
    lo = lax.bitcast_convert_type(x[:, n2:].astype(BF16).astype(F32), U32)
    return hi | (lo >> 16)


def _unpack_rows(p):
    hi = lax.bitcast_convert_type(p & jnp.uint32(0xFFFF0000), F32)
    lo = lax.bitcast_convert_type(p << 16, F32)
    return hi, lo


def _proj_kernel(x_ref, w_ref, o_ref, wbf_ref, *, gelu, n_chunks):
    @pl.when(pl.program_id(1) == 0)
    def _():
        wbf_ref[...] = w_ref[...].astype(BF16)

    rows_per_chunk = x_ref.shape[0] // n_chunks
    for c in range(n_chunks):
        rows = slice(c * rows_per_chunk, (c + 1) * rows_per_chunk)
        acc = _bdot(x_ref[rows, :], wbf_ref[...])
        o_ref[rows, :] = (_gelu(acc) if gelu else acc).astype(o_ref.dtype)


def _proj(x_bf, w, layer, col0, n, gelu, tm, tn):
    t, k = x_bf.shape
    tm, tn = min(tm, t), min(tn, n)
    assert t % tm == 0 and n % tn == 0 and col0 % tn == 0
    return pl.pallas_call(
        functools.partial(_proj_kernel, gelu=gelu, n_chunks=PROJ_CHUNKS),
        grid=(n // tn, t // tm),
        in_specs=[pl.BlockSpec((tm, k), lambda j, i: (i, 0)),
                  pl.BlockSpec((None, k, tn), lambda j, i: (layer, 0, col0 // tn + j),
                               pipeline_mode=pl.Buffered(1))],
        out_specs=pl.BlockSpec((tm, tn), lambda j, i: (i, j)),
        out_shape=jax.ShapeDtypeStruct((t, n), BF16),
        scratch_shapes=[pltpu.VMEM((k, tn), BF16)],
        compiler_params=_cparams(2),
        name="proj",
    )(x_bf, w)


def _sgu_kernel(u_ref, v_ref, g_ref, b_ref, ws_ref, bs_ref, o_ref, *, n_chunks, hd):
    row = lax.broadcasted_iota(I32, (CHUNK, CHUNK), 0)
    col = lax.broadcasted_iota(I32, (CHUNK, CHUNK), 1)
    causal = row >= col
    for c in range(n_chunks):
        rows = slice(c * CHUNK, (c + 1) * CHUNK)
        vn = _layer_norm(v_ref[rows, :].astype(F32), g_ref[...], b_ref[...]).astype(BF16)
        for h in range(N_HEADS):
            cols = slice(h * hd, (h + 1) * hd)
            wm = jnp.where(causal, ws_ref[h], 0.0).astype(BF16)
            sv = _bdot(wm, vn[:, cols]) + bs_ref[:, h:h + 1]
            o_ref[rows, cols] = (u_ref[rows, cols].astype(F32) * sv).astype(o_ref.dtype)


def _sgu(z_bf, ln_g, ln_b, w_s, b_s, rows_per_step):
    t, w2 = z_bf.shape
    w = w2 // 2
    hd = w // N_HEADS
    r = min(rows_per_step, t)
    assert t % r == 0 and r % CHUNK == 0
    return pl.pallas_call(
        functools.partial(_sgu_kernel, n_chunks=r // CHUNK, hd=hd),
        grid=(t // r,),
        in_specs=[pl.BlockSpec((r, w), lambda i: (i, 0)),
                  pl.BlockSpec((r, w), lambda i: (i, 1)),
                  pl.BlockSpec((1, w), lambda i: (0, 0)),
                  pl.BlockSpec((1, w), lambda i: (0, 0)),
                  pl.BlockSpec((N_HEADS, CHUNK, CHUNK), lambda i: (0, 0, 0)),
                  pl.BlockSpec((CHUNK, N_HEADS), lambda i: (0, 0))],
        out_specs=pl.BlockSpec((r, w), lambda i: (i, 0)),
        out_shape=jax.ShapeDtypeStruct((t, w), BF16),
        compiler_params=_cparams(1),
        name="sgu",
    )(z_bf, z_bf, ln_g.reshape(1, w), ln_b.reshape(1, w), w_s, b_s.T)


def _rglru_kernel(y_ref, x_ref, cw_ref, cb_ref, wr_ref, br_ref, wi_ref, bi_ref, lam_ref,
                  o_ref, tail_ref, hc_ref, hs_ref, *, heads_per_step, hd):
    tt, cg = x_ref.shape
    n8 = tt // SUBLANES

    @pl.when(pl.program_id(2) == 0)
    def _():
        tail_ref[...] = jnp.zeros_like(tail_ref)
        hc_ref[...] = jnp.zeros_like(hc_ref)

    sub = lax.broadcasted_iota(I32, (1, SUBLANES, 1), 1)
    xr = x_ref[...].astype(F32)
    x3 = xr.reshape(n8, SUBLANES, cg)
    tail3 = tail_ref[...].reshape(1, SUBLANES, cg)
    xc3 = x3 * cw_ref[CONV_WIDTH - 1:CONV_WIDTH, :] + cb_ref[...]
    for s in range(1, CONV_WIDTH):
        k = CONV_WIDTH - 1 - s
        rot = pltpu.roll(x3, s, 1)
        prev = jnp.concatenate([pltpu.roll(tail3, s, 1), rot[:n8 - 1]], axis=0)
        xc3 = xc3 + jnp.where(sub < s, prev, rot) * cw_ref[k:k + 1, :]
    tail_ref[...] = xr[tt - SUBLANES:, :]
    xc = xc3.reshape(tt, cg)

    xcb = xc.astype(BF16)
    r_parts, i_parts = [], []
    for h in range(heads_per_step):
        cols = slice(h * hd, (h + 1) * hd)
        r_parts.append(_bdot(xcb[:, cols], wr_ref[h].astype(BF16)))
        i_parts.append(_bdot(xcb[:, cols], wi_ref[h].astype(BF16)))
    r = jax.nn.sigmoid(jnp.concatenate(r_parts, axis=1) + br_ref[...])
    ig = jax.nn.sigmoid(jnp.concatenate(i_parts, axis=1) + bi_ref[...])

    lam = lam_ref[...]
    softplus_neg_lam = jnp.maximum(-lam, 0.0) + jnp.log1p(jnp.exp(-jnp.abs(lam)))
    log_a = (-LRU_C * r) * softplus_neg_lam
    a = jnp.exp(log_a)
    th = jnp.tanh(log_a)
    mult = jnp.sqrt(-2.0 * th / (1.0 - th))
    b = mult * (ig * xc)

    a = a.reshape(n8, SUBLANES, cg)
    b = b.reshape(n8, SUBLANES, cg)
    for d in (1, 2, 4):
        keep = sub >= d
        a_prev = jnp.where(keep, pltpu.roll(a, d, 1), 1.0)
        b_prev = jnp.where(keep, pltpu.roll(b, d, 1), 0.0)
        b = a * b_prev + b
        a = a * a_prev

    carry = hc_ref[...]
    for j in range(n8):
        hj = a[j] * carry + b[j]
        hs_ref[j * SUBLANES:(j + 1) * SUBLANES, :] = hj
        carry = jnp.broadcast_to(hj[SUBLANES - 1:SUBLANES, :], hj.shape)
    hc_ref[...] = carry

    o_ref[...] = (hs_ref[...] * y_ref[...].astype(F32)).astype(o_ref.dtype)


def _rglru(y_bf, x_bf, bsz, conv_w, conv_b, w_r, b_r, w_i, b_i, lam, tt, heads_per_step):
    t, w = x_bf.shape
    hd = w // N_HEADS
    seq = t // bsz
    tt = min(tt, seq)
    hps = heads_per_step
    cg = hps * hd
    assert seq % tt == 0 and N_HEADS % hps == 0 and tt % SUBLANES == 0
    n_g, n_s = N_HEADS // hps, seq // tt
    row = lambda g, b, s: b * n_s + s
    vec = lambda g, b, s: (0, g)
    return pl.pallas_call(
        functools.partial(_rglru_kernel, heads_per_step=hps, hd=hd),
        grid=(n_g, bsz, n_s),
        in_specs=[pl.BlockSpec((tt, cg), lambda g, b, s: (row(g, b, s), g)),
                  pl.BlockSpec((tt, cg), lambda g, b, s: (row(g, b, s), g)),
                  pl.BlockSpec((CONV_WIDTH, cg), vec),
                  pl.BlockSpec((1, cg), vec),
                  pl.BlockSpec((hps, hd, hd), lambda g, b, s: (g, 0, 0)),
                  pl.BlockSpec((1, cg), vec),
                  pl.BlockSpec((hps, hd, hd), lambda g, b, s: (g, 0, 0)),
                  pl.BlockSpec((1, cg), vec),
                  pl.BlockSpec((1, cg), vec)],
        out_specs=pl.BlockSpec((tt, cg), lambda g, b, s: (row(g, b, s), g)),
        out_shape=jax.ShapeDtypeStruct((t, w), BF16),
        scratch_shapes=[pltpu.VMEM((SUBLANES, cg), F32),
                        pltpu.VMEM((SUBLANES, cg), F32),
                        pltpu.VMEM((tt, cg), F32)],
        compiler_params=_cparams(3),
        name="rglru",
    )(y_bf, x_bf, conv_w, conv_b.reshape(1, w), w_r, b_r.reshape(1, w),
      w_i, b_i.reshape(1, w), lam.reshape(1, w))


def _out_ln_kernel(x_ref, w_ref, h_ref, g_ref, b_ref, o_ref, *, alpha):
    y = alpha * h_ref[...] + _bdot(x_ref[...], w_ref[...])
    o_ref[...] = _layer_norm(y, g_ref[...], b_ref[...])


def _out_ln(x_bf, w_bf, h, ln_g, ln_b, alpha, tm):
    t, k = x_bf.shape
    d = w_bf.shape[1]
    tm = min(tm, t)
    assert t % tm == 0
    return pl.pallas_call(
        functools.partial(_out_ln_kernel, alpha=alpha),
        grid=(t // tm,),
        in_specs=[pl.BlockSpec((tm, k), lambda i: (i, 0)),
                  pl.BlockSpec((k, d), lambda i: (0, 0), pipeline_mode=pl.Buffered(1)),
                  pl.BlockSpec((tm, d), lambda i: (i, 0)),
                  pl.BlockSpec((1, d), lambda i: (0, 0)),
                  pl.BlockSpec((1, d), lambda i: (0, 0))],
        out_specs=pl.BlockSpec((tm, d), lambda i: (i, 0)),
        out_shape=jax.ShapeDtypeStruct((t, d), F32),
        compiler_params=_cparams(1),
        name="out_ln",
    )(x_bf, w_bf, h, ln_g.reshape(1, d), ln_b.reshape(1, d))


def _router_kernel(h_ref, wt_ref, rb_ref, e_ref, g_ref, p_ref, cnt_ref, carry_ref):
    n_e, tm = wt_ref.shape[0], h_ref.shape[0]

    @pl.when(pl.program_id(0) == 0)
    def _():
        carry_ref[...] = jnp.zeros_like(carry_ref)

    x = h_ref[...]
    xh = x.astype(BF16)
    xl = (x - xh.astype(F32)).astype(BF16)
    w = wt_ref[...]
    wh = w.astype(BF16)
    wl = (w - wh.astype(F32)).astype(BF16)
    dg = lambda a, b: lax.dot_general(a, b, NT_DIMS, preferred_element_type=F32)
    logits = dg(wh, xh) + dg(wh, xl) + dg(wl, xh) + rb_ref[...]

    eidx = lax.broadcasted_iota(I32, (n_e, tm), 0)
    work = logits
    tops, sel = [], []
    for _ in range(TOP_K):
        m = jnp.max(work, axis=0, keepdims=True)
        e = jnp.min(jnp.where(work == m, eidx, n_e), axis=0, keepdims=True)
        hit = eidx == e
        tops.append(m)
        sel.append(hit)
        e_ref[len(sel) - 1:len(sel), :] = e
        work = jnp.where(hit, -jnp.inf, work)

    ps = [jnp.exp(m - tops[0]) for m in tops]
    denom = ps[0] + ps[1] + ps[2] + ps[3]
    for k in range(TOP_K):
        g_ref[k:k + 1, :] = ps[k] / denom

    onehot = jnp.where(sel[0] | sel[1] | sel[2] | sel[3], 1.0, 0.0)
    before = (lax.broadcasted_iota(I32, (tm, tm), 0)
              < lax.broadcasted_iota(I32, (tm, tm), 1)).astype(BF16)
    rank = _bdot(onehot.astype(BF16), before) + carry_ref[:, 0:1]
    for k in range(TOP_K):
        pk = jnp.sum(jnp.where(sel[k], rank, 0.0), axis=0, keepdims=True)
        p_ref[k:k + 1, :] = pk.astype(I32)
    carry_ref[...] = carry_ref[...] + jnp.sum(onehot, axis=1, keepdims=True)
    cnt_ref[...] = carry_ref[...]


def _router(h, router_w, router_b, tm):
    t, d = h.shape
    n_e = router_w.shape[1]
    tm = min(tm, t)
    assert t % tm == 0
    tok = lambda i: (0, i)
    return pl.pallas_call(
        _router_kernel,
        grid=(t // tm,),
        in_specs=[pl.BlockSpec((tm, d), lambda i: (i, 0)),
                  pl.BlockSpec((n_e, d), lambda i: (0, 0)),
                  pl.BlockSpec((n_e, 1), lambda i: (0, 0))],
        out_specs=[pl.BlockSpec((TOP_K, tm), tok),
                   pl.BlockSpec((TOP_K, tm), tok),
                   pl.BlockSpec((TOP_K, tm), tok),
                   pl.BlockSpec((n_e, LANES), lambda i: (0, 0))],
        out_shape=[jax.ShapeDtypeStruct((TOP_K, t), I32),
                   jax.ShapeDtypeStruct((TOP_K, t), F32),
                   jax.ShapeDtypeStruct((TOP_K, t), I32),
                   jax.ShapeDtypeStruct((n_e, LANES), F32)],
        scratch_shapes=[pltpu.VMEM((n_e, LANES), F32)],
        compiler_params=_cparams(1),
        name="router",
    )(h, router_w.T, router_b.reshape(n_e, 1))


def _push_kernel(dest_ref, pad_lo_ref, pad_hi_ref, h_ref, xs_ref, pk_ref, z_ref, sem, zsem,
                 *, tm, n_pad):
    i = pl.program_id(0)
    pk_ref[...] = _pack_rows(h_ref[...]).reshape(pk_ref.shape)

    def issue(j, c):
        for u in range(SUBLANES):
            for k in range(TOP_K):
                dst = dest_ref[0, 0, (j * SUBLANES + u) * TOP_K + k]
                pltpu.make_async_copy(pk_ref.at[j, pl.ds(u, 1)], xs_ref.at[pl.ds(dst, 1)],
                                      sem).start(priority=k % 2)
        return c

    lax.fori_loop(0, tm // SUBLANES, issue, 0)

    for _ in range(SUBLANES * TOP_K):
        pltpu.make_async_copy(pk_ref.at[:, 0], xs_ref.at[pl.ds(0, tm // SUBLANES)], sem).wait()

    @pl.when(i == pl.num_programs(0) - 1)
    def _():
        z_ref[...] = jnp.zeros_like(z_ref)

        def zero_copy(r):
            return pltpu.make_async_copy(z_ref.at[pl.ds(0, 1)], xs_ref.at[pl.ds(r, 1)], zsem)

        for e in range(n_pad):
            lo, hi = pad_lo_ref[e], pad_hi_ref[e]
            lax.fori_loop(lo, hi, lambda r, c: (zero_copy(r).start(), c)[1], 0)
            lax.fori_loop(lo, hi, lambda r, c: (zero_copy(r).wait(), c)[1], 0)


def _push(h, dest3, pad_lo, pad_hi, n_rows):
    t, d = h.shape
    n_steps, tm = dest3.shape[0], dest3.shape[2] // TOP_K
    return pl.pallas_call(
        functools.partial(_push_kernel, tm=tm, n_pad=pad_lo.shape[0]),
        grid=(n_steps,),
        in_specs=[pl.BlockSpec((1, 1, TOP_K * tm), lambda i: (i, 0, 0), memory_space=pltpu.SMEM),
                  pl.BlockSpec(memory_space=pltpu.SMEM),
                  pl.BlockSpec(memory_space=pltpu.SMEM),
                  pl.BlockSpec((tm, d), lambda i: (i, 0))],
        out_specs=pl.BlockSpec(memory_space=pl.ANY),
        out_shape=jax.ShapeDtypeStruct((n_rows, d // 2), U32),
        scratch_shapes=[pltpu.VMEM((tm // SUBLANES, SUBLANES, d // 2), U32),
                        pltpu.VMEM((SUBLANES, d // 2), U32),
                        pltpu.SemaphoreType.DMA, pltpu.SemaphoreType.DMA],
        compiler_params=_cparams(1),
        name="push",
    )(dest3, pad_lo, pad_hi, h)


def _expert_kernel(te_ref, ts_ref, nu_ref, first_ref, nxt_ref, x_ref, wgu_hbm, bgu_ref, wd_hbm,
                   bd_ref, o_ref, wgu_f32, wd_f32, wgu_ref, wd_ref, sem, *, ff, layer):
    i = pl.program_id(0)
    used = i < nu_ref[0]

    def fetch(e):
        return (pltpu.make_async_copy(wgu_hbm.at[layer, e], wgu_f32, sem.at[0]),
                pltpu.make_async_copy(wd_hbm.at[layer, e], wd_f32, sem.at[1]))

    @pl.when(i == 0)
    def _():
        for c in fetch(te_ref[0]):
            c.start()

    @pl.when(first_ref[i] == 1)
    def _():
        for c in fetch(te_ref[i]):
            c.wait()
        wgu_ref[...] = wgu_f32[...].astype(BF16)
        wd_ref[...] = wd_f32[...].astype(BF16)

        @pl.when(nxt_ref[i] >= 0)
        def _():
            for c in fetch(nxt_ref[i]):
                c.start()

    @pl.when(used)
    def _():
        x_hi, x_lo = _unpack_rows(x_ref[...])
        x = jnp.concatenate([x_hi.astype(BF16), x_lo.astype(BF16)], axis=1)
        piece = x.shape[0] // EXPERT_GU_PIECES
        gu = jnp.concatenate([_bdot(x[p * piece:(p + 1) * piece], wgu_ref[...])
                              for p in range(EXPERT_GU_PIECES)], axis=0) + bgu_ref[...]
        hg = jnp.minimum(gu[:, :ff], SWIGLU_LIMIT)
        hu = jnp.clip(gu[:, ff:], -SWIGLU_LIMIT, SWIGLU_LIMIT)
        hdn = hg * jax.nn.sigmoid(SWIGLU_ALPHA * hg) * (hu + 1.0)
        o_ref[...] = _pack_rows(_bdot(hdn.astype(BF16), wd_ref[...]) + bd_ref[...])

    @pl.when(jnp.logical_not(used))
    def _():
        o_ref[...] = jnp.zeros_like(o_ref)


def _experts(xs, tile_e, tile_src, n_used, first, nxt, layer, w_gu, b_gu, w_down, b_down, tr):
    n_rows, d2 = xs.shape
    _, n_e, d, ff2 = w_gu.shape
    ff = ff2 // 2
    n_tiles = n_rows // tr
    grid_spec = pltpu.PrefetchScalarGridSpec(
        num_scalar_prefetch=5,
        grid=(n_tiles,),
        in_specs=[pl.BlockSpec((tr, d2), lambda i, te, ts, *_: (ts[i], 0)),
                  pl.BlockSpec(memory_space=pl.ANY),
                  pl.BlockSpec((None, 1, ff2), lambda i, te, *_: (te[i], 0, 0)),
                  pl.BlockSpec(memory_space=pl.ANY),
                  pl.BlockSpec((None, 1, d), lambda i, te, *_: (te[i], 0, 0))],
        out_specs=pl.BlockSpec((tr, d2), lambda i, *_: (i, 0)),
        scratch_shapes=[pltpu.VMEM((d, ff2), F32), pltpu.VMEM((ff, d), F32),
                        pltpu.VMEM((d, ff2), BF16), pltpu.VMEM((ff, d), BF16),
                        pltpu.SemaphoreType.DMA((2,))],
    )
    return pl.pallas_call(
        functools.partial(_expert_kernel, ff=ff, layer=layer),
        grid_spec=grid_spec,
        out_shape=jax.ShapeDtypeStruct((n_rows, d2), U32),
        compiler_params=_cparams(1),
        name="experts",
    )(tile_e, tile_src, n_used, first, nxt, xs, w_gu, b_gu.reshape(n_e, 1, ff2), w_down,
      b_down.reshape(n_e, 1, d))


def _combine_kernel(dcur_ref, dnext_ref, ys_ref, h_ref, gate_ref, g_ref, b_ref,
                    o_ref, obf_ref, buf_ref, sem, *, alpha, tm):
    i = pl.program_id(0)
    n = pl.num_programs(0)
    slot = i % 2
    d2 = buf_ref.shape[-1]
    grp = COMBINE_GROUP

    gps = grp // SUBLANES

    def start_group(dref, s, j):
        for u in range(grp):
            for k in range(TOP_K):
                src = dref[0, 0, (j * grp + u) * TOP_K + k]
                pltpu.make_async_copy(
                    ys_ref.at[pl.ds(src, 1)],
                    buf_ref.at[s, k, j * gps + u // SUBLANES, pl.ds(u % SUBLANES, 1)],
                    sem.at[s]).start(priority=k % 2)

    def wait_slot(s):
        for k in range(TOP_K):
            for _ in range(SUBLANES):
                pltpu.make_async_copy(ys_ref.at[pl.ds(0, tm // SUBLANES)], buf_ref.at[s, k, :, 0],
                                      sem.at[s]).wait()

    @pl.when(i == 0)
    def _():
        lax.fori_loop(0, tm // grp, lambda j, c: (start_group(dcur_ref, 0, j), c)[1], 0)

    wait_slot(slot)

    def body(j, c):
        r0 = pl.multiple_of(j * grp, grp)
        rows = pl.ds(r0, grp)
        gate = gate_ref[rows, :]
        y_hi = alpha * h_ref[rows, :d2]
        y_lo = alpha * h_ref[rows, d2:]
        for k in range(TOP_K):
            e_hi, e_lo = _unpack_rows(buf_ref[slot, k, pl.ds(j * gps, gps)].reshape(grp, d2))
            y_hi = y_hi + gate[:, k:k + 1] * e_hi
            y_lo = y_lo + gate[:, k:k + 1] * e_lo
        out = _layer_norm(jnp.concatenate([y_hi, y_lo], axis=1), g_ref[...], b_ref[...])
        o_ref[rows, :] = out
        obf_ref[rows, :] = out.astype(obf_ref.dtype)
        start_group(dnext_ref, 1 - slot, j)
        return c

    lax.fori_loop(0, tm // grp, body, 0)

    @pl.when(i == n - 1)
    def _():
        wait_slot(1 - slot)


def _combine(ys, dest3, h, gate_t, ln_g, ln_b, alpha):
    t, d = h.shape
    n_steps, tm = dest3.shape[0], dest3.shape[2] // TOP_K
    smem = lambda f: pl.BlockSpec((1, 1, TOP_K * tm), f, memory_space=pltpu.SMEM)
    return pl.pallas_call(
        functools.partial(_combine_kernel, alpha=alpha, tm=tm),
        grid=(n_steps,),
        in_specs=[smem(lambda i: (i, 0, 0)),
                  smem(lambda i: (jnp.minimum(i + 1, n_steps - 1), 0, 0)),
                  pl.BlockSpec(memory_space=pl.ANY),
                  pl.BlockSpec((tm, d), lambda i: (i, 0)),
                  pl.BlockSpec((tm, TOP_K), lambda i: (i, 0)),
                  pl.BlockSpec((1, d), lambda i: (0, 0)),
                  pl.BlockSpec((1, d), lambda i: (0, 0))],
        out_specs=[pl.BlockSpec((tm, d), lambda i: (i, 0)),
                   pl.BlockSpec((tm, d), lambda i: (i, 0))],
        out_shape=[jax.ShapeDtypeStruct((t, d), F32),
                   jax.ShapeDtypeStruct((t, d), BF16)],
        scratch_shapes=[pltpu.VMEM((2, TOP_K, tm // SUBLANES, SUBLANES, d // 2), U32),
                        pltpu.SemaphoreType.DMA((2,))],
        compiler_params=_cparams(1),
        name="combine",
    )(dest3, dest3, ys, h, gate_t, ln_g.reshape(1, d), ln_b.reshape(1, d))


PROJ_TM, PROJ_TN = 1024, 1024
PROJ_CHUNKS = 4
SGU_ROWS = 512
RGLRU_TT, RGLRU_HEADS = 512, 2
OUT_TM = 128
ROUTER_TM = 512
PUSH_TM = 512
EXPERT_ROWS = 256
EXPERT_GU_PIECES = 2
COMBINE_TM = 256
COMBINE_GROUP = 16


def _blocked_index(a, tm):
    k, t = a.shape
    return a.T.reshape(t // tm, 1, tm * k)


def _moe(h, alpha, layer, router_w, router_b, w_gu, b_gu, w_down, b_down, ln_g, ln_b):
    t, d = h.shape
    n_e = router_w.shape[1]
    tr = EXPERT_ROWS
    top_e, gate, pos, cnt = _router(h, router_w, router_b, ROUTER_TM)

    counts = cnt[:, 0].astype(I32)
    padded = (counts + tr - 1) // tr * tr
    pad_end = jnp.cumsum(padded)
    pad_start = pad_end - padded
    expert_id = jnp.arange(n_e, dtype=I32)[:, None, None]
    dest = pos + jnp.sum(jnp.where(top_e[None] == expert_id, pad_start[:, None, None], 0), axis=0)
    n_tiles = (t * TOP_K + n_e * (tr - 1)) // tr + 1
    n_used = (pad_end[-1] // tr).astype(I32)
    tile_id = jnp.arange(n_tiles, dtype=I32)
    tile_e = jnp.sum((pad_end[None, :] <= (tile_id * tr)[:, None]).astype(I32), axis=1)
    tile_e = jnp.minimum(tile_e, n_e - 1)
    tile_src = jnp.minimum(tile_id, n_used - 1)
    prev_e = jnp.concatenate([jnp.full((1,), -1, I32), tile_e[:-1]])
    first = ((tile_id < n_used) & (tile_e != prev_e)).astype(I32)
    run_end = pad_end[tile_e] // tr
    nxt = jnp.where(run_end < n_used, tile_e[jnp.minimum(run_end, n_tiles - 1)], -1).astype(I32)

    n_rows = n_tiles * tr
    pad_lo = jnp.concatenate([pad_start + counts, pad_end[-1:]]).astype(I32)
    pad_hi = jnp.concatenate([pad_end, jnp.full((1,), n_rows)]).astype(I32)
    xs = _push(h, _blocked_index(dest, min(PUSH_TM, t)), pad_lo, pad_hi, n_rows)
    ys = _experts(xs, tile_e, tile_src, n_used.reshape(1), first, nxt, layer,
                  w_gu, b_gu, w_down, b_down, tr)
    return _combine(ys, _blocked_index(dest, min(COMBINE_TM, t)), h, gate.T, ln_g, ln_b, alpha)


def kernel(x, a_w_in, a_ln_g, a_ln_b, a_w_s, a_b_s, a_w_out, b_w_in, b_conv_w, b_conv_b, b_w_r, b_b_r, b_w_i, b_b_i, b_lambda, b_w_out, ln1_g, ln1_b, ln2_g, ln2_b, router_w, router_b, ex_w_gu, ex_b_gu, ex_w_down, ex_b_down):
    bsz, seq, d = x.shape
    depth = ln1_g.shape[0]
    alpha = (2 * depth) ** 0.25
    h = x.reshape(bsz * seq, d)
    h_bf = h.astype(BF16)
    for layer in range(depth):
        j = layer // 2
        if layer % 2 == 0:
            z = _proj(h_bf, a_w_in, j, 0, 2 * d, True, PROJ_TM, PROJ_TN)
            mixed = _sgu(z, a_ln_g[j], a_ln_b[j], a_w_s[j], a_b_s[j], SGU_ROWS)
            w_out = a_w_out[j]
        else:
            y = _proj(h_bf, b_w_in, j, 0, d, True, PROJ_TM, PROJ_TN)
            xr = _proj(h_bf, b_w_in, j, d, d, False, PROJ_TM, PROJ_TN)
            mixed = _rglru(y, xr, bsz, b_conv_w[j], b_conv_b[j], b_w_r[j], b_b_r[j],
                           b_w_i[j], b_b_i[j], b_lambda[j], RGLRU_TT, RGLRU_HEADS)
            w_out = b_w_out[j]
        h1 = _out_ln(mixed, w_out.astype(BF16), h, ln1_g[layer], ln1_b[layer], alpha, OUT_TM)
        h, h_bf = _moe(h1, alpha, layer, router_w[layer], router_b[layer],
                       ex_w_gu, ex_b_gu[layer], ex_w_down, ex_b_down[layer],
                       ln2_g[layer], ln2_b[layer])
    return h.reshape(bsz, seq, d)
```

```python
import functools

import jax
import jax.numpy as jnp
from jax import lax
from jax.experimental import pallas as pl
from jax.experimental.pallas import tpu as pltpu

F32 = jnp.float32
BF16 = jnp.bfloat16
I32 = jnp.int32
U32 = jnp.uint32

CHUNK = 128
N_HEADS = 16
CONV_WIDTH = 4
LRU_C = 8.0
TOP_K = 4
SWIGLU_LIMIT = 7.0
SWIGLU_ALPHA = 1.702
LN_EPS = 1e-5

LANES = 128
SUBLANES = 8
VMEM_LIMIT_BYTES = 60000 * 1024

NT_DIMS = (((1,), (1,)), ((), ()))


def _cparams(n_axes, vmem=VMEM_LIMIT_BYTES):
    return pltpu.CompilerParams(
        dimension_semantics=("arbitrary",) * n_axes, vmem_limit_bytes=vmem)


def _gelu(x):
    return 0.5 * x * (1.0 + jnp.tanh(0.7978845608028654 * (x + 0.044715 * (x * x * x))))


def _layer_norm(x, g, b):
    mu = jnp.mean(x, axis=-1, keepdims=True)
    xc = x - mu
    var = jnp.mean(xc * xc, axis=-1, keepdims=True)
    return xc * lax.rsqrt(var + LN_EPS) * g + b


def _bdot(a, b):
    return jnp.dot(a, b, preferred_element_type=F32)


def _pack_rows(x):
    n2 = x.shape[1] // 2
    hi = lax.bitcast_convert_type(x[:, :n2].astype(BF16).astype(F32), U32)
    lo = lax.bitcast_convert_type(x[:, n2:].astype(BF16).astype(F32), U32)
    return hi | (lo >> 16)


def _unpack_rows(p):
    hi = lax.bitcast_convert_type(p & jnp.uint32(0xFFFF0000), F32)
    lo = lax.bitcast_convert_type(p << 16, F32)
    return hi, lo


def _proj_kernel(x_ref, w_ref, o_ref, wbf_ref, *, gelu, n_chunks):
    @pl.when(pl.program_id(1) == 0)
    def _():
        wbf_ref[...] = w_ref[...].astype(BF16)

    rows_per_chunk = x_ref.shape[0] // n_chunks
    for c in range(n_chunks):
        rows = slice(c * rows_per_chunk, (c + 1) * rows_per_chunk)
        acc = _bdot(x_ref[rows, :], wbf_ref[...])
        o_ref[rows, :] = (_gelu(acc) if gelu else acc).astype(o_ref.dtype)


def _proj(x_bf, w, layer, col0, n, gelu, tm, tn):
    t, k = x_bf.shape
    tm, tn = min(tm, t), min(tn, n)
    assert t % tm == 0 and n % tn == 0 and col0 % tn == 0
    return pl.pallas_call(
        functools.partial(_proj_kernel, gelu=gelu, n_chunks=PROJ_CHUNKS),
        grid=(n // tn, t // tm),
        in_specs=[pl.BlockSpec((tm, k), lambda j, i: (i, 0)),
                  pl.BlockSpec((None, k, tn), lambda j, i: (layer, 0, col0 // tn + j),
                               pipeline_mode=pl.Buffered(1))],
        out_specs=pl.BlockSpec((tm, tn), lambda j, i: (i, j)),
        out_shape=jax.ShapeDtypeStruct((t, n), BF16),
        scratch_shapes=[pltpu.VMEM((k, tn), BF16)],
        compiler_params=_cparams(2),
        name="proj",
    )(x_bf, w)


def _sgu_kernel(u_ref, v_ref, g_ref, b_ref, ws_ref, bs_ref, o_ref, *, n_chunks, hd):
    row = lax.broadcasted_iota(I32, (CHUNK, CHUNK), 0)
    col = lax.broadcasted_iota(I32, (CHUNK, CHUNK), 1)
    causal = row >= col
    for c in range(n_chunks):
        rows = slice(c * CHUNK, (c + 1) * CHUNK)
        vn = _layer_norm(v_ref[rows, :].astype(F32), g_ref[...], b_ref[...]).astype(BF16)
        for h in range(N_HEADS):
            cols = slice(h * hd, (h + 1) * hd)
            wm = jnp.where(causal, ws_ref[h], 0.0).astype(BF16)
            sv = _bdot(wm, vn[:, cols]) + bs_ref[:, h:h + 1]
            o_ref[rows, cols] = (u_ref[rows, cols].astype(F32) * sv).astype(o_ref.dtype)


def _sgu(z_bf, ln_g, ln_b, w_s, b_s, rows_per_step):
    t, w2 = z_bf.shape
    w = w2 // 2
    hd = w // N_HEADS
    r = min(rows_per_step, t)
    assert t % r == 0 and r % CHUNK == 0
    return pl.pallas_call(
        functools.partial(_sgu_kernel, n_chunks=r // CHUNK, hd=hd),
        grid=(t // r,),
        in_specs=[pl.BlockSpec((r, w), lambda i: (i, 0)),
                  pl.BlockSpec((r, w), lambda i: (i, 1)),
                  pl.BlockSpec((1, w), lambda i: (0, 0)),
                  pl.BlockSpec((1, w), lambda i: (0, 0)),
                  pl.BlockSpec((N_HEADS, CHUNK, CHUNK), lambda i: (0, 0, 0)),
                  pl.BlockSpec((CHUNK, N_HEADS), lambda i: (0, 0))],
        out_specs=pl.BlockSpec((r, w), lambda i: (i, 0)),
        out_shape=jax.ShapeDtypeStruct((t, w), BF16),
        compiler_params=_cparams(1),
        name="sgu",
    )(z_bf, z_bf, ln_g.reshape(1, w), ln_b.reshape(1, w), w_s, b_s.T)


def _rglru_kernel(y_ref, x_ref, cw_ref, cb_ref, wr_ref, br_ref, wi_ref, bi_ref, lam_ref,
                  o_ref, tail_ref, hc_ref, hs_ref, *, heads_per_step, hd):
    tt, cg = x_ref.shape
    n8 = tt // SUBLANES

    @pl.when(pl.program_id(2) == 0)
    def _():
        tail_ref[...] = jnp.zeros_like(tail_ref)
        hc_ref[...] = jnp.zeros_like(hc_ref)

    sub = lax.broadcasted_iota(I32, (1, SUBLANES, 1), 1)
    xr = x_ref[...].astype(F32)
    x3 = xr.reshape(n8, SUBLANES, cg)
    tail3 = tail_ref[...].reshape(1, SUBLANES, cg)
    xc3 = x3 * cw_ref[CONV_WIDTH - 1:CONV_WIDTH, :] + cb_ref[...]
    for s in range(1, CONV_WIDTH):
        k = CONV_WIDTH - 1 - s
        rot = pltpu.roll(x3, s, 1)
        prev = jnp.concatenate([pltpu.roll(tail3, s, 1), rot[:n8 - 1]], axis=0)
        xc3 = xc3 + jnp.where(sub < s, prev, rot) * cw_ref[k:k + 1, :]
    tail_ref[...] = xr[tt - SUBLANES:, :]
    xc = xc3.reshape(tt, cg)

    xcb = xc.astype(BF16)
    r_parts, i_parts = [], []
    for h in range(heads_per_step):
        cols = slice(h * hd, (h + 1) * hd)
        r_parts.append(_bdot(xcb[:, cols], wr_ref[h].astype(BF16)))
        i_parts.append(_bdot(xcb[:, cols], wi_ref[h].astype(BF16)))
    r = jax.nn.sigmoid(jnp.concatenate(r_parts, axis=1) + br_ref[...])
    ig = jax.nn.sigmoid(jnp.concatenate(i_parts, axis=1) + bi_ref[...])

    lam = lam_ref[...]
    softplus_neg_lam = jnp.maximum(-lam, 0.0) + jnp.log1p(jnp.exp(-jnp.abs(lam)))
    log_a = (-LRU_C * r) * softplus_neg_lam
    a = jnp.exp(log_a)
    th = jnp.tanh(log_a)
    mult = jnp.sqrt(-2.0 * th / (1.0 - th))
    b = mult * (ig * xc)

    a = a.reshape(n8, SUBLANES, cg)
    b = b.reshape(n8, SUBLANES, cg)
    for d in (1, 2, 4):
        keep = sub >= d
        a_prev = jnp.where(keep, pltpu.roll(a, d, 1), 1.0)
        b_prev = jnp.where(keep, pltpu.roll(b, d, 1), 0.0)
        b = a * b_prev + b
        a = a * a_prev

    carry = hc_ref[...]
    for j in range(n8):
        hj = a[j] * carry + b[j]
        hs_ref[j * SUBLANES:(j + 1) * SUBLANES, :] = hj
        carry = jnp.broadcast_to(hj[SUBLANES - 1:SUBLANES, :], hj.shape)
    hc_ref[...] = carry

    o_ref[...] = (hs_ref[...] * y_ref[...].astype(F32)).astype(o_ref.dtype)


def _rglru(y_bf, x_bf, bsz, conv_w, conv_b, w_r, b_r, w_i, b_i, lam, tt, heads_per_step):
    t, w = x_bf.shape
    hd = w // N_HEADS
    seq = t // bsz
    tt = min(tt, seq)
    hps = heads_per_step
    cg = hps * hd
    assert seq % tt == 0 and N_HEADS % hps == 0 and tt % SUBLANES == 0
    n_g, n_s = N_HEADS // hps, seq // tt
    row = lambda g, b, s: b * n_s + s
    vec = lambda g, b, s: (0, g)
    return pl.pallas_call(
        functools.partial(_rglru_kernel, heads_per_step=hps, hd=hd),
        grid=(n_g, bsz, n_s),
        in_specs=[pl.BlockSpec((tt, cg), lambda g, b, s: (row(g, b, s), g)),
                  pl.BlockSpec((tt, cg), lambda g, b, s: (row(g, b, s), g)),
                  pl.BlockSpec((CONV_WIDTH, cg), vec),
                  pl.BlockSpec((1, cg), vec),
                  pl.BlockSpec((hps, hd, hd), lambda g, b, s: (g, 0, 0)),
                  pl.BlockSpec((1, cg), vec),
                  pl.BlockSpec((hps, hd, hd), lambda g, b, s: (g, 0, 0)),
                  pl.BlockSpec((1, cg), vec),
                  pl.BlockSpec((1, cg), vec)],
        out_specs=pl.BlockSpec((tt, cg), lambda g, b, s: (row(g, b, s), g)),
        out_shape=jax.ShapeDtypeStruct((t, w), BF16),
        scratch_shapes=[pltpu.VMEM((SUBLANES, cg), F32),
                        pltpu.VMEM((SUBLANES, cg), F32),
                        pltpu.VMEM((tt, cg), F32)],
        compiler_params=_cparams(3),
        name="rglru",
    )(y_bf, x_bf, conv_w, conv_b.reshape(1, w), w_r, b_r.reshape(1, w),
      w_i, b_i.reshape(1, w), lam.reshape(1, w))


def _out_ln_kernel(x_ref, w_ref, h_ref, g_ref, b_ref, o_ref, *, alpha):
    y = alpha * h_ref[...] + _bdot(x_ref[...], w_ref[...])
    o_ref[...] = _layer_norm(y, g_ref[...], b_ref[...])


def _out_ln(x_bf, w_bf, h, ln_g, ln_b, alpha, tm):
    t, k = x_bf.shape
    d = w_bf.shape[1]
    tm = min(tm, t)
    assert t % tm == 0
    return pl.pallas_call(
        functools.partial(_out_ln_kernel, alpha=alpha),
        grid=(t // tm,),
        in_specs=[pl.BlockSpec((tm, k), lambda i: (i, 0)),
                  pl.BlockSpec((k, d), lambda i: (0, 0), pipeline_mode=pl.Buffered(1)),
                  pl.BlockSpec((tm, d), lambda i: (i, 0)),
                  pl.BlockSpec((1, d), lambda i: (0, 0)),
                  pl.BlockSpec((1, d), lambda i: (0, 0))],
        out_specs=pl.BlockSpec((tm, d), lambda i: (i, 0)),
        out_shape=jax.ShapeDtypeStruct((t, d), F32),
        compiler_params=_cparams(1),
        name="out_ln",
    )(x_bf, w_bf, h, ln_g.reshape(1, d), ln_b.reshape(1, d))


def _router_kernel(h_ref, wt_ref, rb_ref, e_ref, g_ref, p_ref, cnt_ref, carry_ref):
    n_e, tm = wt_ref.shape[0], h_ref.shape[0]

    @pl.when(pl.program_id(0) == 0)
    def _():
        carry_ref[...] = jnp.zeros_like(carry_ref)

    x = h_ref[...]
    xh = x.astype(BF16)
    xl = (x - xh.astype(F32)).astype(BF16)
    w = wt_ref[...]
    wh = w.astype(BF16)
    wl = (w - wh.astype(F32)).astype(BF16)
    dg = lambda a, b: lax.dot_general(a, b, NT_DIMS, preferred_element_type=F32)
    logits = dg(wh, xh) + dg(wh, xl) + dg(wl, xh) + rb_ref[...]

    eidx = lax.broadcasted_iota(I32, (n_e, tm), 0)
    work = logits
    tops, sel = [], []
    for _ in range(TOP_K):
        m = jnp.max(work, axis=0, keepdims=True)
        e = jnp.min(jnp.where(work == m, eidx, n_e), axis=0, keepdims=True)
        hit = eidx == e
        tops.append(m)
        sel.append(hit)
        e_ref[len(sel) - 1:len(sel), :] = e
        work = jnp.where(hit, -jnp.inf, work)

    ps = [jnp.exp(m - tops[0]) for m in tops]
    denom = ps[0] + ps[1] + ps[2] + ps[3]
    for k in range(TOP_K):
        g_ref[k:k + 1, :] = ps[k] / denom

    onehot = jnp.where(sel[0] | sel[1] | sel[2] | sel[3], 1.0, 0.0)
    before = (lax.broadcasted_iota(I32, (tm, tm), 0)
              < lax.broadcasted_iota(I32, (tm, tm), 1)).astype(BF16)
    rank = _bdot(onehot.astype(BF16), before) + carry_ref[:, 0:1]
    for k in range(TOP_K):
        pk = jnp.sum(jnp.where(sel[k], rank, 0.0), axis=0, keepdims=True)
        p_ref[k:k + 1, :] = pk.astype(I32)
    carry_ref[...] = carry_ref[...] + jnp.sum(onehot, axis=1, keepdims=True)
    cnt_ref[...] = carry_ref[...]


def _router(h, router_w, router_b, tm):
    t, d = h.shape
    n_e = router_w.shape[1]
    tm = min(tm, t)
    assert t % tm == 0
    tok = lambda i: (0, i)
    return pl.pallas_call(
        _router_kernel,
        grid=(t // tm,),
        in_specs=[pl.BlockSpec((tm, d), lambda i: (i, 0)),
                  pl.BlockSpec((n_e, d), lambda i: (0, 0)),
                  pl.BlockSpec((n_e, 1), lambda i: (0, 0))],
        out_specs=[pl.BlockSpec((TOP_K, tm), tok),
                   pl.BlockSpec((TOP_K, tm), tok),
                   pl.BlockSpec((TOP_K, tm), tok),
                   pl.BlockSpec((n_e, LANES), lambda i: (0, 0))],
        out_shape=[jax.ShapeDtypeStruct((TOP_K, t), I32),
                   jax.ShapeDtypeStruct((TOP_K, t), F32),
                   jax.ShapeDtypeStruct((TOP_K, t), I32),
                   jax.ShapeDtypeStruct((n_e, LANES), F32)],
        scratch_shapes=[pltpu.VMEM((n_e, LANES), F32)],
        compiler_params=_cparams(1),
        name="router",
    )(h, router_w.T, router_b.reshape(n_e, 1))


def _push_kernel(dest_ref, pad_lo_ref, pad_hi_ref, h_ref, xs_ref, pk_ref, z_ref, sem, zsem,
                 *, tm, n_pad):
    i = pl.program_id(0)
    pk_ref[...] = _pack_rows(h_ref[...]).reshape(pk_ref.shape)

    def issue(j, c):
        for u in range(SUBLANES):
            for k in range(TOP_K):
                dst = dest_ref[0, 0, (j * SUBLANES + u) * TOP_K + k]
                pltpu.make_async_copy(pk_ref.at[j, pl.ds(u, 1)], xs_ref.at[pl.ds(dst, 1)],
                                      sem).start(priority=k % 2)
        return c

    lax.fori_loop(0, tm // SUBLANES, issue, 0)

    for _ in range(SUBLANES * TOP_K):
        pltpu.make_async_copy(pk_ref.at[:, 0], xs_ref.at[pl.ds(0, tm // SUBLANES)], sem).wait()

    @pl.when(i == pl.num_programs(0) - 1)
    def _():
        z_ref[...] = jnp.zeros_like(z_ref)

        def zero_copy(r):
            return pltpu.make_async_copy(z_ref.at[pl.ds(0, 1)], xs_ref.at[pl.ds(r, 1)], zsem)

        for e in range(n_pad):
            lo, hi = pad_lo_ref[e], pad_hi_ref[e]
            lax.fori_loop(lo, hi, lambda r, c: (zero_copy(r).start(), c)[1], 0)
            lax.fori_loop(lo, hi, lambda r, c: (zero_copy(r).wait(), c)[1], 0)


def _push(h, dest3, pad_lo, pad_hi, n_rows):
    t, d = h.shape
    n_steps, tm = dest3.shape[0], dest3.shape[2] // TOP_K
    return pl.pallas_call(
        functools.partial(_push_kernel, tm=tm, n_pad=pad_lo.shape[0]),
        grid=(n_steps,),
        in_specs=[pl.BlockSpec((1, 1, TOP_K * tm), lambda i: (i, 0, 0), memory_space=pltpu.SMEM),
                  pl.BlockSpec(memory_space=pltpu.SMEM),
                  pl.BlockSpec(memory_space=pltpu.SMEM),
                  pl.BlockSpec((tm, d), lambda i: (i, 0))],
        out_specs=pl.BlockSpec(memory_space=pl.ANY),
        out_shape=jax.ShapeDtypeStruct((n_rows, d // 2), U32),
        scratch_shapes=[pltpu.VMEM((tm // SUBLANES, SUBLANES, d // 2), U32),
                        pltpu.VMEM((SUBLANES, d // 2), U32),
                        pltpu.SemaphoreType.DMA, pltpu.SemaphoreType.DMA],
        compiler_params=_cparams(1),
        name="push",
    )(dest3, pad_lo, pad_hi, h)


def _expert_kernel(te_ref, ts_ref, nu_ref, first_ref, nxt_ref, x_ref, wgu_hbm, bgu_ref, wd_hbm,
                   bd_ref, o_ref, wgu_f32, wd_f32, wgu_ref, wd_ref, sem, *, ff, layer):
    i = pl.program_id(0)
    used = i < nu_ref[0]

    def fetch(e):
        return (pltpu.make_async_copy(wgu_hbm.at[layer, e], wgu_f32, sem.at[0]),
                pltpu.make_async_copy(wd_hbm.at[layer, e], wd_f32, sem.at[1]))

    @pl.when(i == 0)
    def _():
        for c in fetch(te_ref[0]):
            c.start()

    @pl.when(first_ref[i] == 1)
    def _():
        for c in fetch(te_ref[i]):
            c.wait()
        wgu_ref[...] = wgu_f32[...].astype(BF16)
        wd_ref[...] = wd_f32[...].astype(BF16)

        @pl.when(nxt_ref[i] >= 0)
        def _():
            for c in fetch(nxt_ref[i]):
                c.start()

    @pl.when(used)
    def _():
        x_hi, x_lo = _unpack_rows(x_ref[...])
        x = jnp.concatenate([x_hi.astype(BF16), x_lo.astype(BF16)], axis=1)
        piece = x.shape[0] // EXPERT_GU_PIECES
        gu = jnp.concatenate([_bdot(x[p * piece:(p + 1) * piece], wgu_ref[...])
                              for p in range(EXPERT_GU_PIECES)], axis=0) + bgu_ref[...]
        hg = jnp.minimum(gu[:, :ff], SWIGLU_LIMIT)
        hu = jnp.clip(gu[:, ff:], -SWIGLU_LIMIT, SWIGLU_LIMIT)
        hdn = hg * jax.nn.sigmoid(SWIGLU_ALPHA * hg) * (hu + 1.0)
        hb = hdn.astype(BF16)
        for p in range(EXPERT_GU_PIECES):
            rows = slice(p * piece, (p + 1) * piece)
            o_ref[rows, :] = _pack_rows(_bdot(hb[rows], wd_ref[...]) + bd_ref[...])

    @pl.when(jnp.logical_not(used))
    def _():
        o_ref[...] = jnp.zeros_like(o_ref)


def _experts(xs, tile_e, tile_src, n_used, first, nxt, layer, w_gu, b_gu, w_down, b_down, tr):
    n_rows, d2 = xs.shape
    _, n_e, d, ff2 = w_gu.shape
    ff = ff2 // 2
    n_tiles = n_rows // tr
    grid_spec = pltpu.PrefetchScalarGridSpec(
        num_scalar_prefetch=5,
        grid=(n_tiles,),
        in_specs=[pl.BlockSpec((tr, d2), lambda i, te, ts, *_: (ts[i], 0)),
                  pl.BlockSpec(memory_space=pl.ANY),
                  pl.BlockSpec((None, 1, ff2), lambda i, te, *_: (te[i], 0, 0)),
                  pl.BlockSpec(memory_space=pl.ANY),
                  pl.BlockSpec((None, 1, d), lambda i, te, *_: (te[i], 0, 0))],
        out_specs=pl.BlockSpec((tr, d2), lambda i, *_: (i, 0)),
        scratch_shapes=[pltpu.VMEM((d, ff2), F32), pltpu.VMEM((ff, d), F32),
                        pltpu.VMEM((d, ff2), BF16), pltpu.VMEM((ff, d), BF16),
                        pltpu.SemaphoreType.DMA((2,))],
    )
    return pl.pallas_call(
        functools.partial(_expert_kernel, ff=ff, layer=layer),
        grid_spec=grid_spec,
        out_shape=jax.ShapeDtypeStruct((n_rows, d2), U32),
        compiler_params=_cparams(1),
        name="experts",
    )(tile_e, tile_src, n_used, first, nxt, xs, w_gu, b_gu.reshape(n_e, 1, ff2), w_down,
      b_down.reshape(n_e, 1, d))


def _combine_kernel(dcur_ref, dnext_ref, ys_ref, h_ref, gate_ref, g_ref, b_ref,
                    o_ref, obf_ref, buf_ref, sem, *, alpha, tm):
    i = pl.program_id(0)
    n = pl.num_programs(0)
    slot = i % 2
    d2 = buf_ref.shape[-1]
    grp = COMBINE_GROUP

    gps = grp // SUBLANES

    def start_group(dref, s, j):
        for u in range(grp):
            for k in range(TOP_K):
                src = dref[0, 0, (j * grp + u) * TOP_K + k]
                pltpu.make_async_copy(
                    ys_ref.at[pl.ds(src, 1)],
                    buf_ref.at[s, k, j * gps + u // SUBLANES, pl.ds(u % SUBLANES, 1)],
                    sem.at[s]).start(priority=k % 2)

    def wait_slot(s):
        for k in range(TOP_K):
            for _ in range(SUBLANES):
                pltpu.make_async_copy(ys_ref.at[pl.ds(0, tm // SUBLANES)], buf_ref.at[s, k, :, 0],
                                      sem.at[s]).wait()

    @pl.when(i == 0)
    def _():
        lax.fori_loop(0, tm // grp, lambda j, c: (start_group(dcur_ref, 0, j), c)[1], 0)

    wait_slot(slot)

    def body(j, c):
        r0 = pl.multiple_of(j * grp, grp)
        rows = pl.ds(r0, grp)
        gate = gate_ref[rows, :]
        y_hi = alpha * h_ref[rows, :d2]
        y_lo = alpha * h_ref[rows, d2:]
        for k in range(TOP_K):
            e_hi, e_lo = _unpack_rows(buf_ref[slot, k, pl.ds(j * gps, gps)].reshape(grp, d2))
            y_hi = y_hi + gate[:, k:k + 1] * e_hi
            y_lo = y_lo + gate[:, k:k + 1] * e_lo
        out = _layer_norm(jnp.concatenate([y_hi, y_lo], axis=1), g_ref[...], b_ref[...])
        o_ref[rows, :] = out
        obf_ref[rows, :] = out.astype(obf_ref.dtype)
        start_group(dnext_ref, 1 - slot, j)
        return c

    lax.fori_loop(0, tm // grp, body, 0)

    @pl.when(i == n - 1)
    def _():
        wait_slot(1 - slot)


def _combine(ys, dest3, h, gate_t, ln_g, ln_b, alpha):
    t, d = h.shape
    n_steps, tm = dest3.shape[0], dest3.shape[2] // TOP_K
    smem = lambda f: pl.BlockSpec((1, 1, TOP_K * tm), f, memory_space=pltpu.SMEM)
    return pl.pallas_call(
        functools.partial(_combine_kernel, alpha=alpha, tm=tm),
        grid=(n_steps,),
        in_specs=[smem(lambda i: (i, 0, 0)),
                  smem(lambda i: (jnp.minimum(i + 1, n_steps - 1), 0, 0)),
                  pl.BlockSpec(memory_space=pl.ANY),
                  pl.BlockSpec((tm, d), lambda i: (i, 0)),
                  pl.BlockSpec((tm, TOP_K), lambda i: (i, 0)),
                  pl.BlockSpec((1, d), lambda i: (0, 0)),
                  pl.BlockSpec((1, d), lambda i: (0, 0))],
        out_specs=[pl.BlockSpec((tm, d), lambda i: (i, 0)),
                   pl.BlockSpec((tm, d), lambda i: (i, 0))],
        out_shape=[jax.ShapeDtypeStruct((t, d), F32),
                   jax.ShapeDtypeStruct((t, d), BF16)],
        scratch_shapes=[pltpu.VMEM((2, TOP_K, tm // SUBLANES, SUBLANES, d // 2), U32),
                        pltpu.SemaphoreType.DMA((2,))],
        compiler_params=_cparams(1),
        name="combine",
    )(dest3, dest3, ys, h, gate_t, ln_g.reshape(1, d), ln_b.reshape(1, d))


PROJ_TM, PROJ_TN = 1024, 1024
PROJ_CHUNKS = 4
SGU_ROWS = 512
RGLRU_TT, RGLRU_HEADS = 512, 4
OUT_TM = 128
ROUTER_TM = 512
PUSH_TM = 512
EXPERT_ROWS = 256
EXPERT_GU_PIECES = 2
COMBINE_TM = 256
COMBINE_GROUP = 16


def _blocked_index(a, tm):
    k, t = a.shape
    return a.T.reshape(t // tm, 1, tm * k)


def _moe(h, alpha, layer, router_w, router_b, w_gu, b_gu, w_down, b_down, ln_g, ln_b):
    t, d = h.shape
    n_e = router_w.shape[1]
    tr = EXPERT_ROWS
    top_e, gate, pos, cnt = _router(h, router_w, router_b, ROUTER_TM)

    counts = cnt[:, 0].astype(I32)
    padded = (counts + tr - 1) // tr * tr
    pad_end = jnp.cumsum(padded)
    pad_start = pad_end - padded
    expert_id = jnp.arange(n_e, dtype=I32)[:, None, None]
    dest = pos + jnp.sum(jnp.where(top_e[None] == expert_id, pad_start[:, None, None], 0), axis=0)
    n_tiles = (t * TOP_K + n_e * (tr - 1)) // tr + 1
    n_used = (pad_end[-1] // tr).astype(I32)
    tile_id = jnp.arange(n_tiles, dtype=I32)
    tile_e = jnp.sum((pad_end[None, :] <= (tile_id * tr)[:, None]).astype(I32), axis=1)
    tile_e = jnp.minimum(tile_e, n_e - 1)
    tile_src = jnp.minimum(tile_id, n_used - 1)
    prev_e = jnp.concatenate([jnp.full((1,), -1, I32), tile_e[:-1]])
    first = ((tile_id < n_used) & (tile_e != prev_e)).astype(I32)
    run_end = pad_end[tile_e] // tr
    nxt = jnp.where(run_end < n_used, tile_e[jnp.minimum(run_end, n_tiles - 1)], -1).astype(I32)

    n_rows = n_tiles * tr
    pad_lo = jnp.concatenate([pad_start + counts, pad_end[-1:]]).astype(I32)
    pad_hi = jnp.concatenate([pad_end, jnp.full((1,), n_rows)]).astype(I32)
    xs = _push(h, _blocked_index(dest, min(PUSH_TM, t)), pad_lo, pad_hi, n_rows)
    ys = _experts(xs, tile_e, tile_src, n_used.reshape(1), first, nxt, layer,
                  w_gu, b_gu, w_down, b_down, tr)
    return _combine(ys, _blocked_index(dest, min(COMBINE_TM, t)), h, gate.T, ln_g, ln_b, alpha)


def kernel(x, a_w_in, a_ln_g, a_ln_b, a_w_s, a_b_s, a_w_out, b_w_in, b_conv_w, b_conv_b, b_w_r, b_b_r, b_w_i, b_b_i, b_lambda, b_w_out, ln1_g, ln1_b, ln2_g, ln2_b, router_w, router_b, ex_w_gu, ex_b_gu, ex_w_down, ex_b_down):
    bsz, seq, d = x.shape
    depth = ln1_g.shape[0]
    alpha = (2 * depth) ** 0.25
    h = x.reshape(bsz * seq, d)
    h_bf = h.astype(BF16)
    for layer in range(depth):
        j = layer // 2
        if layer % 2 == 0:
            z = _proj(h_bf, a_w_in, j, 0, 2 * d, True, PROJ_TM, PROJ_TN)
            mixed = _sgu(z, a_ln_g[j], a_ln_b[j], a_w_s[j], a_b_s[j], SGU_ROWS)
            w_out = a_w_out[j]
        else:
            y = _proj(h_bf, b_w_in, j, 0, d, True, PROJ_TM, PROJ_TN)
            xr = _proj(h_bf, b_w_in, j, d, d, False, PROJ_TM, PROJ_TN)
            mixed = _rglru(y, xr, bsz, b_conv_w[j], b_conv_b[j], b_w_r[j], b_b_r[j],
                           b_w_i[j], b_b_i[j], b_lambda[j], RGLRU_TT, RGLRU_HEADS)
            w_out = b_w_out[j]
        h1 = _out_ln(mixed, w_out.astype(BF16), h, ln1_g[layer], ln1_b[layer], alpha, OUT_TM)
        h, h_bf = _moe(h1, alpha, layer, router_w[layer], router_b[layer],
                       ex_w_gu, ex_b_gu[layer], ex_w_down, ex_b_down[layer],
                       ln2_g[layer], ln2_b[layer])
    return h.reshape(bsz, seq, d)
```
